```python
import jax, jax.numpy as jnp
from jax import lax
import numpy as np

D_MODEL = 1024
BATCH = 8
SEQ = 2048
DEPTH = 1
DEC_BATCH = 128
DEC_SEQ = 4
PAST_LEN = 16384
PAGE_SIZE = 128

MIX_WIDTH = 2 * D_MODEL
SSD_WIDTH = MIX_WIDTH // 2
SSD_HEAD_DIM = 64
SSD_HEADS = SSD_WIDTH // SSD_HEAD_DIM
SSD_GROUPS = 2
SSD_STATE = 64
SSD_CHUNK = 128
SSD_CONV_DIM = SSD_WIDTH + 2 * SSD_GROUPS * SSD_STATE
LRU_WIDTH = MIX_WIDTH - SSD_WIDTH
LRU_BLOCKS = 16
LRU_BLOCK_DIM = LRU_WIDTH // LRU_BLOCKS
LRU_C = 8.0
CONV_W = 4
IN_PROJ_DIM = SSD_WIDTH + SSD_CONV_DIM + SSD_HEADS + 2 * LRU_WIDTH
PEER_HEADS = 8
PEER_KEYS = 128
PEER_EXPERTS = PEER_KEYS * PEER_KEYS
PEER_TOPK = 16
PEER_QDIM = 256
PEER_HALF = PEER_QDIM // 2
PEER_BLOCK = 256
PLE_DIM = 256
EPS = 1e-6

kernel_name = 'hymba_ssd_rglru_peer_step'


def rmsnorm(x, g):
    xf = x.astype(jnp.float32)
    out = xf * lax.rsqrt(jnp.mean(xf * xf, axis=-1, keepdims=True) + EPS)
    return (out * g.astype(jnp.float32)).astype(x.dtype)


def causal_conv(x, buf, w, b):
    L = x.shape[1]
    xp = jnp.concatenate([buf.astype(x.dtype), x], axis=1)
    y = b + xp[:, 0:L] * w[0]
    for k in range(1, CONV_W):
        y = y + xp[:, k:k + L] * w[k]
    return y, xp[:, -(CONV_W - 1):]


def segsum(x):
    T = x.shape[-1]
    cs = jnp.cumsum(x, axis=-1)
    diff = cs[..., :, None] - cs[..., None, :]
    mask = jnp.tril(jnp.ones((T, T), dtype=bool))
    return jnp.where(mask, diff, -jnp.inf)


def ssd_chunked(xs, dt, a, bm, cm, h0):
    b, l, H, P = xs.shape
    q = min(SSD_CHUNK, l)
    c = l // q
    rep = H // SSD_GROUPS
    bh = jnp.repeat(bm, rep, axis=2).reshape(b, c, q, H, SSD_STATE)
    ch = jnp.repeat(cm, rep, axis=2).reshape(b, c, q, H, SSD_STATE)
    xdt = (xs * dt[..., None]).reshape(b, c, q, H, P)
    da = (dt * a).reshape(b, c, q, H).transpose(0, 3, 1, 2)
    a_cs = jnp.cumsum(da, axis=-1)
    scores = jnp.einsum('bclhn,bcshn->bhcls', ch, bh) * jnp.exp(segsum(da))
    y_diag = jnp.einsum('bhcls,bcshp->bclhp', scores, xdt)
    decay_states = jnp.exp(a_cs[..., -1:] - a_cs)
    states = jnp.einsum('bclhn,bhcl,bclhp->bchpn', bh, decay_states, xdt)
    states = jnp.concatenate([h0.astype(states.dtype)[:, None], states], axis=1)
    chunk_decay = jnp.exp(segsum(jnp.pad(a_cs[..., -1], ((0, 0), (0, 0), (1, 0)))))
    new_states = jnp.einsum('bhzc,bchpn->bzhpn', chunk_decay, states)
    prev_states, final = new_states[:, :-1], new_states[:, -1]
    y_off = jnp.einsum('bclhn,bchpn,bhcl->bclhp', ch, prev_states, jnp.exp(a_cs))
    return (y_diag + y_off).reshape(b, l, H, P), final


def lru_scan(a_t, bx, h0):
    bx = bx.at[:, 0].add(a_t[:, 0] * h0.astype(bx.dtype))

    def comb(c1, c2):
        a1, b1 = c1
        a2, b2 = c2
        return a1 * a2, a2 * b1 + b2

    _, hs = lax.associative_scan(comb, (a_t, bx), axis=1)
    return hs, hs[:, -1]


def peer(xn, wq, sub_keys, expert_u, expert_v):
    B_, L_, D = xn.shape
    t = xn.reshape(-1, D)
    T = t.shape[0]
    nb = -(-T // PEER_BLOCK)
    t = jnp.pad(t, ((0, nb * PEER_BLOCK - T), (0, 0))).reshape(nb, PEER_BLOCK, D)

    def block(tb):
        n = tb.shape[0]
        qv = (tb @ wq).reshape(n, PEER_HEADS, 2, PEER_HALF)
        s = jnp.einsum('thcd,hckd->thck', qv, sub_keys).astype(jnp.float32)
        s_top, i_top = lax.top_k(s, PEER_TOPK)
        cand = (s_top[:, :, 0, :, None] + s_top[:, :, 1, None, :]).reshape(n, PEER_HEADS, -1)
        cand_idx = (i_top[:, :, 0, :, None] * PEER_KEYS + i_top[:, :, 1, None, :]).reshape(n, PEER_HEADS, -1)
        best, pos = lax.top_k(cand, PEER_TOPK)
        idx = jnp.take_along_axis(cand_idx, pos, axis=-1)
        g = jax.nn.softmax(best, axis=-1).astype(tb.dtype)
        u = expert_u[idx]
        act = jax.nn.gelu(jnp.einsum('td,thkd->thk', tb, u))
        v = expert_v[idx]
        return jnp.einsum('thk,thkd->td', g * act, v)

    out = lax.map(block, t).reshape(-1, D)[:T]
    return out.reshape(B_, L_, D)


def layer(h, p, ssd_h0, ssd_buf, lru_h0, lru_buf, lw):
    b, l, _ = h.shape
    xn = rmsnorm(h, lw['norm_mix'])
    proj = xn @ lw['w_in']
    s1 = SSD_WIDTH
    s2 = s1 + SSD_CONV_DIM
    s3 = s2 + SSD_HEADS
    s4 = s3 + LRU_WIDTH
    z, xbc, dt, gate, xl = jnp.split(proj, [s1, s2, s3, s4], axis=-1)
    xbc, ssd_buf_new = causal_conv(xbc, ssd_buf, lw['conv_ssd_w'], lw['conv_ssd_b'])
    xbc = jax.nn.silu(xbc)
    xs, bm, cm = jnp.split(xbc, [SSD_WIDTH, SSD_WIDTH + SSD_GROUPS * SSD_STATE], axis=-1)
    xs = xs.reshape(b, l, SSD_HEADS, SSD_HEAD_DIM)
    bm = bm.reshape(b, l, SSD_GROUPS, SSD_STATE)
    cm = cm.reshape(b, l, SSD_GROUPS, SSD_STATE)
    dt = jax.nn.softplus(dt + lw['dt_bias'])
    a = -jnp.exp(lw['a_log'])
    y, ssd_h = ssd_chunked(xs, dt, a, bm, cm, ssd_h0)
    y = y + lw['d_skip'][:, None] * xs
    y_ssd = rmsnorm(y.reshape(b, l, SSD_WIDTH) * jax.nn.silu(z), lw['norm_ssd'])
    xl, lru_buf_new = causal_conv(xl, lru_buf, lw['conv_lru_w'], lw['conv_lru_b'])
    xb = xl.reshape(b, l, LRU_BLOCKS, LRU_BLOCK_DIM)
    r_gate = jax.nn.sigmoid(jnp.einsum('blki,kij->blkj', xb, lw['w_rgate']).reshape(b, l, LRU_WIDTH) + lw['b_rgate'])
    i_gate = jax.nn.sigmoid(jnp.einsum('blki,kij->blkj', xb, lw['w_igate']).reshape(b, l, LRU_WIDTH) + lw['b_igate'])
    log_a = -LRU_C * r_gate * jax.nn.softplus(-lw['lru_lambda'])
    a_t = jnp.exp(log_a)
    mult = jnp.sqrt(-jnp.expm1(2.0 * log_a))
    hs, lru_h = lru_scan(a_t, mult * (i_gate * xl), lru_h0)
    y_lru = rmsnorm(hs * jax.nn.gelu(gate), lw['norm_lru'])
    h = h + jnp.concatenate([y_ssd, y_lru], axis=-1) @ lw['w_out']
    h = h + peer(rmsnorm(h, lw['norm_ffn']), lw['peer_wq'], lw['peer_keys'], lw['peer_u'], lw['peer_v'])
    g = jax.nn.sigmoid(rmsnorm(h, lw['norm_ple_gate']) @ lw['w_ple_gate'])
    h = h + rmsnorm((p @ lw['w_ple_proj']) * g, lw['norm_ple_out'])
    return h, ssd_h, ssd_buf_new, lru_h, lru_buf_new


def setup_inputs(seed: int = 0) -> dict:
    key = jax.random.key(seed)
    ks = jax.random.split(key, 40)
    f32 = jnp.float32

    def nrm(k, shape, scale):
        return jax.random.normal(k, shape, f32) * scale

    def gain(k, shape):
        return 1.0 + 0.02 * jax.random.normal(k, shape, f32)

    u_dt = jax.random.uniform(ks[0], (DEPTH, SSD_HEADS), f32)
    dt0 = jnp.exp(u_dt * (np.log(0.1) - np.log(0.001)) + np.log(0.001))
    dt_bias = dt0 + jnp.log(-jnp.expm1(-dt0))
    a_log = jnp.log(jax.random.uniform(ks[1], (DEPTH, SSD_HEADS), f32, minval=1.0, maxval=16.0))
    u_lam = jax.random.uniform(ks[2], (DEPTH, LRU_WIDTH), f32, minval=0.9, maxval=0.999)
    a_base = u_lam ** (1.0 / LRU_C)
    lru_lambda = jnp.log(a_base) - jnp.log1p(-a_base)

    return {
        'x_prompt': nrm(ks[3], (BATCH, SEQ, D_MODEL), 1.0),
        'x_sample': nrm(ks[4], (DEC_BATCH, DEC_SEQ, D_MODEL), 1.0),
        'state_ssd': nrm(ks[5], (DEPTH, DEC_BATCH, SSD_HEADS, SSD_HEAD_DIM, SSD_STATE), 0.5),
        'state_ssd_conv': nrm(ks[6], (DEPTH, DEC_BATCH, CONV_W - 1, SSD_CONV_DIM), 1.0),
        'state_lru': nrm(ks[7], (DEPTH, DEC_BATCH, LRU_WIDTH), 0.5),
        'state_lru_conv': nrm(ks[8], (DEPTH, DEC_BATCH, CONV_W - 1, LRU_WIDTH), 1.0),
        'p_prompt': nrm(ks[9], (DEPTH, BATCH, SEQ, PLE_DIM), 1.0),
        'p_sample': nrm(ks[10], (DEPTH, DEC_BATCH, DEC_SEQ, PLE_DIM), 1.0),
        'norm_mix': gain(ks[11], (DEPTH, D_MODEL)),
        'w_in': nrm(ks[12], (DEPTH, D_MODEL, IN_PROJ_DIM), D_MODEL ** -0.5),
        'conv_ssd_w': nrm(ks[13], (DEPTH, CONV_W, SSD_CONV_DIM), CONV_W ** -0.5),
        'conv_ssd_b': nrm(ks[14], (DEPTH, SSD_CONV_DIM), 0.02),
        'dt_bias': dt_bias,
        'a_log': a_log,
        'd_skip': gain(ks[15], (DEPTH, SSD_HEADS)),
        'norm_ssd': gain(ks[16], (DEPTH, SSD_WIDTH)),
        'conv_lru_w': nrm(ks[17], (DEPTH, CONV_W, LRU_WIDTH), CONV_W ** -0.5),
        'conv_lru_b': nrm(ks[18], (DEPTH, LRU_WIDTH), 0.02),
        'w_rgate': nrm(ks[19], (DEPTH, LRU_BLOCKS, LRU_BLOCK_DIM, LRU_BLOCK_DIM), LRU_BLOCK_DIM ** -0.5),
        'b_rgate': nrm(ks[20], (DEPTH, LRU_WIDTH), 0.02),
        'w_igate': nrm(ks[21], (DEPTH, LRU_BLOCKS, LRU_BLOCK_DIM, LRU_BLOCK_DIM), LRU_BLOCK_DIM ** -0.5),
        'b_igate': nrm(ks[22], (DEPTH, LRU_WIDTH), 0.02),
        'lru_lambda': lru_lambda,
        'norm_lru': gain(ks[23], (DEPTH, LRU_WIDTH)),
        'w_out': nrm(ks[24], (DEPTH, MIX_WIDTH, D_MODEL), MIX_WIDTH ** -0.5),
        'norm_ffn': gain(ks[25], (DEPTH, D_MODEL)),
        'peer_wq': nrm(ks[26], (DEPTH, D_MODEL, PEER_HEADS * PEER_QDIM), D_MODEL ** -0.5),
        'peer_keys': nrm(ks[27], (DEPTH, PEER_HEADS, 2, PEER_KEYS, PEER_HALF), PEER_HALF ** -0.5),
        'peer_u': nrm(ks[28], (DEPTH, PEER_EXPERTS, D_MODEL), D_MODEL ** -0.5),
        'peer_v': nrm(ks[29], (DEPTH, PEER_EXPERTS, D_MODEL), PEER_HEADS ** -0.5),
        'norm_ple_gate': gain(ks[30], (DEPTH, D_MODEL)),
        'w_ple_gate': nrm(ks[31], (DEPTH, D_MODEL, D_MODEL), D_MODEL ** -0.5),
        'w_ple_proj': nrm(ks[32], (DEPTH, PLE_DIM, D_MODEL), PLE_DIM ** -0.5),
        'norm_ple_out': gain(ks[33], (DEPTH, D_MODEL)),
        'norm_final': gain(ks[34], (D_MODEL,)),
    }


def reference(x_prompt, x_sample, state_ssd, state_ssd_conv, state_lru, state_lru_conv,
              p_prompt, p_sample, norm_mix, w_in, conv_ssd_w, conv_ssd_b, dt_bias, a_log,
              d_skip, norm_ssd, conv_lru_w, conv_lru_b, w_rgate, b_rgate, w_igate, b_igate,
              lru_lambda, norm_lru, w_out, norm_ffn, peer_wq, peer_keys, peer_u, peer_v,
              norm_ple_gate, w_ple_gate, w_ple_proj, norm_ple_out, norm_final):
    hp, hs = x_prompt, x_sample
    dt_ = x_prompt.dtype
    sp_ssd, sp_sconv, sp_lru, sp_lconv = [], [], [], []
    ss_ssd, ss_sconv, ss_lru, ss_lconv = [], [], [], []
    for i in range(DEPTH):
        lw = {
            'norm_mix': norm_mix[i], 'w_in': w_in[i],
            'conv_ssd_w': conv_ssd_w[i], 'conv_ssd_b': conv_ssd_b[i],
            'dt_bias': dt_bias[i], 'a_log': a_log[i], 'd_skip': d_skip[i], 'norm_ssd': norm_ssd[i],
            'conv_lru_w': conv_lru_w[i], 'conv_lru_b': conv_lru_b[i],
            'w_rgate': w_rgate[i], 'b_rgate': b_rgate[i], 'w_igate': w_igate[i], 'b_igate': b_igate[i],
            'lru_lambda': lru_lambda[i], 'norm_lru': norm_lru[i], 'w_out': w_out[i],
            'norm_ffn': norm_ffn[i], 'peer_wq': peer_wq[i], 'peer_keys': peer_keys[i],
            'peer_u': peer_u[i], 'peer_v': peer_v[i],
            'norm_ple_gate': norm_ple_gate[i], 'w_ple_gate': w_ple_gate[i],
            'w_ple_proj': w_ple_proj[i], 'norm_ple_out': norm_ple_out[i],
        }
        bp = hp.shape[0]
        hp, a1, a2, a3, a4 = layer(
            hp, p_prompt[i],
            jnp.zeros((bp, SSD_HEADS, SSD_HEAD_DIM, SSD_STATE), dt_),
            jnp.zeros((bp, CONV_W - 1, SSD_CONV_DIM), dt_),
            jnp.zeros((bp, LRU_WIDTH), dt_),
            jnp.zeros((bp, CONV_W - 1, LRU_WIDTH), dt_),
            lw)
        sp_ssd.append(a1); sp_sconv.append(a2); sp_lru.append(a3); sp_lconv.append(a4)
        hs, b1, b2, b3, b4 = layer(hs, p_sample[i], state_ssd[i], state_ssd_conv[i],
                                   state_lru[i], state_lru_conv[i], lw)
        ss_ssd.append(b1); ss_sconv.append(b2); ss_lru.append(b3); ss_lconv.append(b4)
    y_prompt = rmsnorm(hp, norm_final)
    y_sample = rmsnorm(hs, norm_final)
    ssd_state_p = jnp.stack(sp_ssd)
    ssd_conv_p = jnp.stack(sp_sconv)
    lru_state_p = jnp.stack(sp_lru)
    lru_conv_p = jnp.stack(sp_lconv)
    ssd_state_s = jnp.stack(ss_ssd)
    ssd_conv_s = jnp.stack(ss_sconv)
    lru_state_s = jnp.stack(ss_lru)
    lru_conv_s = jnp.stack(ss_lconv)
    return (y_prompt, y_sample, ssd_state_p, ssd_conv_p, lru_state_p, lru_conv_p,
            ssd_state_s, ssd_conv_s, lru_state_s, lru_conv_s)
```

```python
import functools
import math

import jax
import jax.numpy as jnp
from jax import lax
from jax.experimental import pallas as pl
from jax.experimental.pallas import tpu as pltpu

F32 = jnp.float32
BF16 = jnp.bfloat16

D_MODEL = 1024
SSD_WIDTH = 1024
SSD_HEADS = 16
SSD_HEAD_DIM = 64
SSD_STATE = 64
SSD_GROUPS = 2
SSD_CONV_DIM = 1280
LRU_WIDTH = 1024
LRU_BLOCKS = 16
LRU_C = 8.0
CONV_W = 4
PEER_HEADS = 8
PEER_KEYS = 128
PEER_EXPERTS = PEER_KEYS * PEER_KEYS
PEER_TOPK = 16
PLE_DIM = 256
EPS = 1e-6

LANES = 128
SUBLANES = 8
VMEM_LIMIT = 56 * 1024 * 1024

P_Z = 0
P_XBC = P_Z + SSD_WIDTH
P_DT = P_XBC + SSD_CONV_DIM
P_GATE = P_DT + LANES
P_XL = P_GATE + LRU_WIDTH
P_ALL = P_XL + LRU_WIDTH


def _rms(x, g):
    return x * lax.rsqrt(jnp.mean(x * x, axis=-1, keepdims=True) + EPS) * g


def _sigmoid(x):
    return 1.0 / (1.0 + jnp.exp(-x))


def _silu(x):
    return x * _sigmoid(x)


def _gelu(x):
    c = math.sqrt(2.0 / math.pi)
    return 0.5 * x * (1.0 + jnp.tanh(c * (x + 0.044715 * (x * x * x))))


def _softplus(x):
    return jnp.maximum(x, 0.0) + jnp.log1p(jnp.exp(-jnp.abs(x)))


def _expm1(x):
    u = jnp.exp(x)
    near = (u - 1.0) * x / jnp.log(u)
    return jnp.where(u == 1.0, x, jnp.where(x < -1.0, u - 1.0, near))


def _dot(a, b):
    return jnp.dot(a, b, preferred_element_type=F32)


def _dot_nt(a, b):
    return lax.dot_general(a, b, (((1,), (1,)), ((), ())), preferred_element_type=F32)


def _split3(x):
    hi = x.astype(BF16)
    r = x - hi.astype(F32)
    mid = r.astype(BF16)
    lo = (r - mid.astype(F32)).astype(BF16)
    return hi, mid, lo


def _dot3(x, m):
    hi, mid, lo = _split3(x)
    return _dot(hi, m) + _dot(mid, m) + _dot(lo, m)


def _dot3_left(m, x):
    hi, mid, lo = _split3(x)
    return _dot(m, hi) + _dot(m, mid) + _dot(m, lo)


def _const_spec(shape):
    nd = len(shape)
    return pl.BlockSpec(shape, lambda *_: (0,) * nd, pipeline_mode=pl.Buffered(1))


PEER_TT = 512
PEER_EC = 1024
N_EC = PEER_EXPERTS // PEER_EC
I_PER_EC = PEER_EC // PEER_KEYS


def _peer_topk_tile(s1_ref, s2_ref, s1p_ref, ap_ref, b_ref, thr_ref, t1_ref, t2_ref, lane0):
    ninf = -jnp.inf
    ls = pl.ds(lane0, LANES)
    for h in range(PEER_HEADS):
        rs = slice(h * PEER_KEYS, (h + 1) * PEER_KEYS)
        for (s_ref, t_ref) in ((s1_ref, t1_ref), (s2_ref, t2_ref)):
            s = s_ref[rs, ls]
            for r in range(PEER_TOPK):
                m = jnp.max(s, axis=0, keepdims=True)
                t_ref[r:r + 1, :] = m
                s = jnp.where(s == m, ninf, s)
        t1 = t1_ref[...]
        t2 = t2_ref[...]
        parts = [t1[0:1, :] + t2]
        for k1 in range(1, 8):
            parts.append(t1[k1:k1 + 1, :] + t2[0:8, :])
        parts.append(t1[8:16, :] + t2[0:1, :])
        cand = jnp.concatenate(parts, axis=0)
        top = t1[0:1, :] + t2[0:1, :]
        rank = jnp.zeros((1, LANES), F32)
        z = jnp.zeros((1, LANES), F32)
        thr = top
        for r in range(PEER_TOPK):
            m = jnp.max(cand, axis=0, keepdims=True)
            hit = cand == m
            cnt = jnp.sum(jnp.where(hit, 1.0, 0.0), axis=0, keepdims=True)
            live = rank < float(PEER_TOPK)
            z = z + jnp.where(live, cnt * jnp.exp(m - top), 0.0)
            thr = jnp.where(live, m, thr)
            rank = rank + cnt
            cand = jnp.where(hit, ninf, cand)
        thr_ref[h:h + 1, ls] = thr
        s1 = s1_ref[rs, ls]
        a = jnp.exp(s1 - t1[0:1, :]) * (1.0 / z)
        b_ref[rs, ls] = jnp.exp(s2_ref[rs, ls] - t2[0:1, :])
        for c in range(N_EC):
            src = slice(c * I_PER_EC, (c + 1) * I_PER_EC)
            dst = slice((c * PEER_HEADS + h) * I_PER_EC, (c * PEER_HEADS + h + 1) * I_PER_EC)
            s1p_ref[dst, ls] = s1[src, :]
            ap_ref[dst, ls] = a[src, :]


def _peer_kernel(h1_ref, p_ref, nffn_ref, wq_ref, keys_ref, u_ref, vt_ref,
                 npg_ref, wpg_ref, wpp_ref, npo_ref, nfin_ref,
                 y_ref,
                 xn_ref, s1_ref, s2_ref, s1p_ref, ap_ref, b_ref, thr_ref, t1_ref, t2_ref,
                 ht_ref, wg_ref, acc_ref):
    e = pl.program_id(1)
    tt = xn_ref.shape[0]

    @pl.when(e == 0)
    def _():
        xn = _rms(h1_ref[...], nffn_ref[...]).astype(BF16)
        xn_ref[...] = xn
        q = _dot(xn, wq_ref[...]).astype(BF16)
        for h in range(PEER_HEADS):
            for c, s_ref in ((0, s1_ref), (1, s2_ref)):
                hc = h * 2 + c
                s_ref[h * PEER_KEYS:(h + 1) * PEER_KEYS, :] = _dot_nt(
                    keys_ref[hc], q[:, hc * LANES:(hc + 1) * LANES])

        def tile(j, carry):
            _peer_topk_tile(s1_ref, s2_ref, s1p_ref, ap_ref, b_ref, thr_ref, t1_ref, t2_ref,
                            pl.multiple_of(j * LANES, LANES))
            return carry
        lax.fori_loop(0, tt // LANES, tile, 0)
        acc_ref[...] = jnp.zeros_like(acc_ref)

    ht_ref[...] = _dot_nt(u_ref[...], xn_ref[...])

    def per_lane_tile(lt, carry):
        ls = pl.ds(pl.multiple_of(lt * LANES, LANES), LANES)
        for il in range(I_PER_EC):
            es = slice(il * PEER_KEYS, (il + 1) * PEER_KEYS)
            w = jnp.zeros((PEER_KEYS, LANES), F32)
            for h in range(PEER_HEADS):
                rs = slice(h * PEER_KEYS, (h + 1) * PEER_KEYS)
                base = pl.multiple_of((e * PEER_HEADS + h) * I_PER_EC, I_PER_EC)
                s1row = s1p_ref[pl.ds(base, I_PER_EC), ls][il:il + 1, :]
                arow = ap_ref[pl.ds(base, I_PER_EC), ls][il:il + 1, :]
                sel = (s1row + s2_ref[rs, ls]) >= thr_ref[h:h + 1, ls]
                w = w + jnp.where(sel, b_ref[rs, ls], 0.0) * arow
            g = _gelu(ht_ref[es, ls])
            wg_ref[es, ls] = (w * g).astype(BF16)
        return carry
    lax.fori_loop(0, tt // LANES, per_lane_tile, 0)

    acc_ref[...] += _dot(vt_ref[...], wg_ref[...])

    @pl.when(e == N_EC - 1)
    def _():
        h2 = h1_ref[...] + acc_ref[...].T
        g = _sigmoid(_dot(_rms(h2, npg_ref[...]).astype(BF16), wpg_ref[...]))
        pe = _dot(p_ref[...].astype(BF16), wpp_ref[...])
        h3 = h2 + _rms(pe * g, npo_ref[...])
        y_ref[...] = _rms(h3, nfin_ref[...])


def _peer_ple(h1, p, w):
    t = h1.shape[0]
    tt = PEER_TT
    assert t % tt == 0
    grid = (t // tt, N_EC)
    rows = PEER_HEADS * PEER_KEYS
    return pl.pallas_call(
        _peer_kernel,
        grid=grid,
        in_specs=[
            pl.BlockSpec((tt, D_MODEL), lambda i, e: (i, 0)),
            pl.BlockSpec((tt, PLE_DIM), lambda i, e: (i, 0)),
            _const_spec((1, D_MODEL)),
            _const_spec((D_MODEL, 2 * rows)),
            _const_spec((2 * PEER_HEADS, PEER_KEYS, LANES)),
            pl.BlockSpec((PEER_EC, D_MODEL), lambda i, e: (e, 0)),
            pl.BlockSpec((D_MODEL, PEER_EC), lambda i, e: (0, e)),
            _const_spec((1, D_MODEL)),
            _const_spec((D_MODEL, D_MODEL)),
            _const_spec((PLE_DIM, D_MODEL)),
            _const_spec((1, D_MODEL)),
            _const_spec((1, D_MODEL)),
        ],
        out_specs=pl.BlockSpec((tt, D_MODEL), lambda i, e: (i, 0)),
        out_shape=jax.ShapeDtypeStruct((t, D_MODEL), F32),
        scratch_shapes=[
            pltpu.VMEM((tt, D_MODEL), BF16),
            pltpu.VMEM((rows, tt), F32),
            pltpu.VMEM((rows, tt), F32),
            pltpu.VMEM((rows, tt), F32),
            pltpu.VMEM((rows, tt), F32),
            pltpu.VMEM((rows, tt), F32),
            pltpu.VMEM((PEER_HEADS, tt), F32),
            pltpu.VMEM((PEER_TOPK, LANES), F32),
            pltpu.VMEM((PEER_TOPK, LANES), F32),
            pltpu.VMEM((PEER_EC, tt), F32),
            pltpu.VMEM((PEER_EC, tt), BF16),
            pltpu.VMEM((D_MODEL, tt), F32),
        ],
        compiler_params=pltpu.CompilerParams(
            dimension_semantics=("arbitrary", "arbitrary"),
            vmem_limit_bytes=VMEM_LIMIT),
        name="peer_ple",
    )(h1, p, w["norm_ffn"], w["wq"], w["keys"], w["u"], w["vt"],
      w["norm_ple_gate"], w["w_ple_gate"], w["w_ple_proj"], w["norm_ple_out"], w["norm_final"])


def _row(v):
    return v.reshape(1, -1).astype(F32)


def _prep_peer_weights(a):
    return {
        "norm_ffn": _row(a["norm_ffn"][0]),
        "wq": a["peer_wq"][0].astype(BF16),
        "keys": a["peer_keys"][0].reshape(2 * PEER_HEADS, PEER_KEYS, LANES).astype(BF16),
        "u": a["peer_u"][0].astype(BF16),
        "vt": a["peer_v"][0].astype(BF16).T,
        "norm_ple_gate": _row(a["norm_ple_gate"][0]),
        "w_ple_gate": a["w_ple_gate"][0].astype(BF16),
        "w_ple_proj": a["w_ple_proj"][0].astype(BF16),
        "norm_ple_out": _row(a["norm_ple_out"][0]),
        "norm_final": _row(a["norm_final"]),
    }


CHUNK = 128
MIX_TM = 256
HIST = SUBLANES


def _conv_from_ext(ext_ref, w, b, n):
    y = b + ext_ref[HIST - 3:HIST - 3 + n, :] * w[0:1, :]
    y = y + ext_ref[HIST - 2:HIST - 2 + n, :] * w[1:2, :]
    y = y + ext_ref[HIST - 1:HIST - 1 + n, :] * w[2:3, :]
    return y + ext_ref[HIST:HIST + n, :] * w[3:4, :]


def _lru_coeffs(xl, gates):
    wr_ref, br_ref, wi_ref, bi_ref, lam_ref = gates
    xlb = xl.astype(BF16)
    r = _sigmoid(_dot(xlb, wr_ref[...]) + br_ref[...])
    ig = _sigmoid(_dot(xlb, wi_ref[...]) + bi_ref[...])
    log_a = -LRU_C * r * _softplus(-lam_ref[...])
    a_t = jnp.exp(log_a)
    mult = jnp.sqrt(-_expm1(2.0 * log_a))
    return a_t, mult * (ig * xl)


def _ssd_chunk(xbc, dt, a, st_ref, consts, dsk):
    tri_ref, exp16_ref, expall_ref, bdmask_ref = consts
    tri = tri_ref[...]
    da = dt * a
    cs = _dot3_left(tri, da)
    cs_e = _dot3(cs, exp16_ref[...])
    dt_e = _dot3(dt, exp16_ref[...])
    xs = xbc[:, 0:SSD_WIDTH]
    xdt = xs * dt_e
    bmat = xbc[:, SSD_WIDTH:SSD_WIDTH + LANES]
    cmat = xbc[:, SSD_WIDTH + LANES:SSD_WIDTH + 2 * LANES]
    last_e = cs_e[CHUNK - 1:CHUNK, :]
    xd = (xdt * jnp.exp(last_e - cs_e)).astype(BF16)
    st = st_ref[...]
    y_off = _dot(cmat.astype(BF16), st.astype(BF16)) * jnp.exp(cs_e)
    st_ref[...] = st * jnp.exp(last_e) + _dot(bmat.T.astype(BF16), xd) * bdmask_ref[...]

    col_b = _dot3(cs, expall_ref[...])
    cs_t = cs.T
    lane = lax.broadcasted_iota(jnp.int32, (CHUNK, LANES), 1)
    row = lax.broadcasted_iota(jnp.int32, (CHUNK, LANES), 0)
    causal = row >= lane
    low = lane < SSD_STATE
    bmat_b = bmat.astype(BF16)
    cb = [_dot_nt(jnp.where(low, cmat, 0.0).astype(BF16), bmat_b),
          _dot_nt(jnp.where(low, 0.0, cmat).astype(BF16), bmat_b)]
    xdt_b = xdt.astype(BF16)
    pieces = []
    for hp in range(SSD_HEADS // 2):
        cols = slice(hp * LANES, (hp + 1) * LANES)
        outs = []
        for h in (2 * hp, 2 * hp + 1):
            seg = jnp.exp(col_b[:, h * LANES:(h + 1) * LANES] - cs_t[h:h + 1, :])
            m = (cb[h // (SSD_HEADS // SSD_GROUPS)] * jnp.where(causal, seg, 0.0)).astype(BF16)
            outs.append(_dot(m, xdt_b[:, cols]))
        pieces.append(jnp.where(low, outs[0], outs[1]))
    y_diag = jnp.concatenate(pieces, axis=1)
    return y_diag + y_off + dsk * xs


def _mixer_out(x, y, z, y_lru, nssd_ref, wout_ref):
    y_ssd = _rms(y * _silu(z), nssd_ref[...])
    return (x + _dot(y_ssd.astype(BF16), wout_ref[0:SSD_WIDTH, :])
            + _dot(y_lru.astype(BF16), wout_ref[SSD_WIDTH:SSD_WIDTH + LRU_WIDTH, :]))


def _mix_p_kernel(x_ref, nmix_ref, wall_ref, cws_ref, cbs_ref, dtb_ref, alog_ref, dsk_ref, nssd_ref,
                  cwl_ref, cbl_ref, wr_ref, br_ref, wi_ref, bi_ref, lam_ref, nlru_ref, wout_ref,
                  tri_ref, exp16_ref, expall_ref, bdmask_ref,
                  h1_ref, st_out_ref, sconv_out_ref, lstate_out_ref, lconv_out_ref,
                  exts_ref, extl_ref, st_ref, hcar_ref, y_ref, pa_ref, hb_ref):
    l = pl.program_id(1)
    tm = x_ref.shape[0]

    @pl.when(l == 0)
    def _():
        exts_ref[0:HIST, :] = jnp.zeros((HIST, SSD_CONV_DIM), F32)
        extl_ref[0:HIST, :] = jnp.zeros((HIST, LRU_WIDTH), F32)
        st_ref[...] = jnp.zeros_like(st_ref)
        hcar_ref[...] = jnp.zeros_like(hcar_ref)

    x = x_ref[...]
    xn = _rms(x, nmix_ref[...]).astype(BF16)
    proj = _dot(xn, wall_ref[...])
    z = proj[:, P_Z:P_XBC]
    gate = proj[:, P_GATE:P_XL]

    exts_ref[HIST:HIST + tm, :] = proj[:, P_XBC:P_DT]
    xbc = _silu(_conv_from_ext(exts_ref, cws_ref[...], cbs_ref[...], tm))
    exts_ref[0:HIST, :] = exts_ref[tm:tm + HIST, :]
    dt = _softplus(proj[:, P_DT:P_GATE] + dtb_ref[...])
    a = -jnp.exp(alog_ref[...])
    consts = (tri_ref, exp16_ref, expall_ref, bdmask_ref)
    for c in range(tm // CHUNK):
        rs = slice(c * CHUNK, (c + 1) * CHUNK)
        y_ref[rs, :] = _ssd_chunk(xbc[rs, :], dt[rs, :], a, st_ref, consts, dsk_ref[...])

    extl_ref[HIST:HIST + tm, :] = proj[:, P_XL:P_ALL]
    xl = _conv_from_ext(extl_ref, cwl_ref[...], cbl_ref[...], tm)
    extl_ref[0:HIST, :] = extl_ref[tm:tm + HIST, :]
    a_t, bx = _lru_coeffs(xl, (wr_ref, br_ref, wi_ref, bi_ref, lam_ref))
    seg = tm // SUBLANES
    hs_cols = []
    for j in range(LRU_WIDTH // LANES):
        cols = slice(j * LANES, (j + 1) * LANES)
        pa_ref[j] = a_t[:, cols]
        hb_ref[j] = bx[:, cols]
        hloc = jnp.zeros((SUBLANES, LANES), F32)
        ploc = jnp.ones((SUBLANES, LANES), F32)
        for tau in range(seg):
            rows = pl.ds(tau, SUBLANES, stride=seg)
            a_r = pa_ref[j, rows, :]
            hloc = a_r * hloc + hb_ref[j, rows, :]
            ploc = a_r * ploc
            hb_ref[j, rows, :] = hloc
            pa_ref[j, rows, :] = ploc
        carry = hcar_ref[0:1, cols]
        parts = []
        for s in range(SUBLANES):
            rs = slice(s * seg, (s + 1) * seg)
            parts.append(hb_ref[j, rs, :] + pa_ref[j, rs, :] * carry)
            carry = hloc[s:s + 1, :] + ploc[s:s + 1, :] * carry
        hcar_ref[0:1, cols] = carry
        hs_cols.append(jnp.concatenate(parts, axis=0))
    hs = jnp.concatenate(hs_cols, axis=1)
    y_lru = _rms(hs * _gelu(gate), nlru_ref[...])

    h1_ref[...] = _mixer_out(x, y_ref[...], z, y_lru, nssd_ref, wout_ref)

    @pl.when(l == pl.num_programs(1) - 1)
    def _():
        st_out_ref[0] = st_ref[...]
        sconv_out_ref[0] = exts_ref[0:HIST, :]
        lstate_out_ref[0] = hcar_ref[...]
        lconv_out_ref[0] = extl_ref[0:HIST, :]


def _mix_prompt(x, w):
    b, l, _ = x.shape
    tm = MIX_TM
    assert l % tm == 0
    nl = l // tm
    names = ["norm_mix", "w_all", "conv_ssd_w", "conv_ssd_b", "dt_bias", "a_log", "d_skip", "norm_ssd",
             "conv_lru_w", "conv_lru_b", "w_r", "b_r", "w_i", "b_i", "lam", "norm_lru", "w_out",
             "tri", "exp16", "expall", "bdmask"]
    ws = [w[n] for n in names]
    gs = SSD_GROUPS * SSD_STATE
    out_shape = (
        jax.ShapeDtypeStruct((b * l, D_MODEL), F32),
        jax.ShapeDtypeStruct((b, gs, SSD_WIDTH), F32),
        jax.ShapeDtypeStruct((b, HIST, SSD_CONV_DIM), F32),
        jax.ShapeDtypeStruct((b, SUBLANES, LRU_WIDTH), F32),
        jax.ShapeDtypeStruct((b, HIST, LRU_WIDTH), F32),
    )
    per_b = lambda shape: pl.BlockSpec((1,) + shape, lambda i, j: (i, 0, 0))
    return pl.pallas_call(
        _mix_p_kernel,
        grid=(b, nl),
        in_specs=[pl.BlockSpec((tm, D_MODEL), lambda i, j: (i * nl + j, 0))]
        + [_const_spec(v.shape) for v in ws],
        out_specs=(
            pl.BlockSpec((tm, D_MODEL), lambda i, j: (i * nl + j, 0)),
            per_b((gs, SSD_WIDTH)), per_b((HIST, SSD_CONV_DIM)),
            per_b((SUBLANES, LRU_WIDTH)), per_b((HIST, LRU_WIDTH)),
        ),
        out_shape=out_shape,
        scratch_shapes=[
            pltpu.VMEM((HIST + tm, SSD_CONV_DIM), F32),
            pltpu.VMEM((HIST + tm, LRU_WIDTH), F32),
            pltpu.VMEM((gs, SSD_WIDTH), F32),
            pltpu.VMEM((SUBLANES, LRU_WIDTH), F32),
            pltpu.VMEM((tm, SSD_WIDTH), F32),
            pltpu.VMEM((LRU_WIDTH // LANES, tm, LANES), F32),
            pltpu.VMEM((LRU_WIDTH // LANES, tm, LANES), F32),
        ],
        compiler_params=pltpu.CompilerParams(
            dimension_semantics=("arbitrary", "arbitrary"),
            vmem_limit_bytes=VMEM_LIMIT),
        name="mix_prompt",
    )(x.reshape(b * l, D_MODEL), *ws)


NS = 4
NB = 128


def _mix_s1_kernel(x_ref, bufs_ref, bufl_ref, lst_ref, nmix_ref, wall_ref, cws_ref, cbs_ref, dtb_ref,
                   cwl_ref, cbl_ref, wr_ref, br_ref, wi_ref, bi_ref, lam_ref, nlru_ref,
                   z_ref, xst_ref, bct_ref, dtt_ref, ylru_ref, sconv_ref, lstate_ref, lconv_ref):
    xn = _rms(x_ref[...], nmix_ref[...]).astype(BF16)
    proj = _dot(xn, wall_ref[...])
    z_ref[...] = proj[:, P_Z:P_XBC]

    def blocks(buf_ref, raw):
        return ([buf_ref[j * NB:(j + 1) * NB, :] for j in range(CONV_W - 1)]
                + [raw[t * NB:(t + 1) * NB, :] for t in range(NS)])

    def conv(xp, w, b, t):
        y = b + xp[t] * w[0:1, :]
        for k in range(1, CONV_W):
            y = y + xp[t + k] * w[k:k + 1, :]
        return y

    xp = blocks(bufs_ref, proj[:, P_XBC:P_DT])
    for j in range(CONV_W - 1):
        sconv_ref[j * NB:(j + 1) * NB, :] = xp[NS + j]
    for t in range(NS):
        xbc = _silu(conv(xp, cws_ref[...], cbs_ref[...], t))
        dt = _softplus(proj[t * NB:(t + 1) * NB, P_DT:P_GATE] + dtb_ref[...])
        xst_ref[t] = xbc[:, 0:SSD_WIDTH].T
        bct_ref[t] = xbc[:, SSD_WIDTH:SSD_CONV_DIM].T
        dtt_ref[t] = dt.T

    xp = blocks(bufl_ref, proj[:, P_XL:P_ALL])
    for j in range(CONV_W - 1):
        lconv_ref[j * NB:(j + 1) * NB, :] = xp[NS + j]
    h = lst_ref[...]
    for t in range(NS):
        rs = slice(t * NB, (t + 1) * NB)
        xl = conv(xp, cwl_ref[...], cbl_ref[...], t)
        a_t, bx = _lru_coeffs(xl, (wr_ref, br_ref, wi_ref, bi_ref, lam_ref))
        h = a_t * h + bx
        ylru_ref[rs, :] = _rms(h * _gelu(proj[rs, P_GATE:P_XL]), nlru_ref[...])
    lstate_ref[...] = h


def _mix_s1(x_tm, bufs_tm, bufl_tm, lstate, w):
    n = NS * NB
    names = ["norm_mix", "w_all", "conv_ssd_w", "conv_ssd_b", "dt_bias",
             "conv_lru_w", "conv_lru_b", "w_r", "b_r", "w_i", "b_i", "lam", "norm_lru"]
    out_shape = (
        jax.ShapeDtypeStruct((n, SSD_WIDTH), F32),
        jax.ShapeDtypeStruct((NS, SSD_WIDTH, NB), F32),
        jax.ShapeDtypeStruct((NS, SSD_CONV_DIM - SSD_WIDTH, NB), F32),
        jax.ShapeDtypeStruct((NS, LANES, NB), F32),
        jax.ShapeDtypeStruct((n, LRU_WIDTH), F32),
        jax.ShapeDtypeStruct(((CONV_W - 1) * NB, SSD_CONV_DIM), F32),
        jax.ShapeDtypeStruct((NB, LRU_WIDTH), F32),
        jax.ShapeDtypeStruct(((CONV_W - 1) * NB, LRU_WIDTH), F32),
    )
    return pl.pallas_call(
        _mix_s1_kernel, out_shape=out_shape,
        compiler_params=pltpu.CompilerParams(vmem_limit_bytes=VMEM_LIMIT),
        name="mix_sample_in",
    )(x_tm, bufs_tm, bufl_tm, lstate, *[w[k] for k in names])


def _ssd_s_kernel(xs_ref, bc_ref, dt_ref, alog_ref, dsk_ref, s_ref, y_ref, ns_ref):
    h = pl.program_id(0)
    g = h // (SSD_HEADS // SSD_GROUPS)
    b_rows = pl.ds(pl.multiple_of(g * SSD_STATE, SSD_STATE), SSD_STATE)
    c_rows = pl.ds(pl.multiple_of(LANES + g * SSD_STATE, SSD_STATE), SSD_STATE)
    a = -jnp.exp(alog_ref[0])
    ns_ref[...] = s_ref[...]
    for t in range(NS):
        dtr = dt_ref[t, 0]
        decay = jnp.exp(dtr * a)
        bt = bc_ref[t, b_rows, :]
        ct = bc_ref[t, c_rows, :]
        xs = xs_ref[t]
        xdt = xs * dtr
        for p in range(SSD_HEAD_DIM):
            s_p = decay * ns_ref[0, p] + xdt[p:p + 1, :] * bt
            ns_ref[0, p] = s_p
            y_ref[t, p:p + 1, :] = jnp.sum(ct * s_p, axis=0, keepdims=True)
        y_ref[t] = y_ref[t] + dsk_ref[0] * xs


def _ssd_sample(xst, bct, dtt, alog_b, dsk_b, state_bm):
    return pl.pallas_call(
        _ssd_s_kernel,
        grid=(SSD_HEADS,),
        in_specs=[
            pl.BlockSpec((NS, SSD_HEAD_DIM, NB), lambda h: (0, h, 0)),
            pl.BlockSpec((NS, SSD_CONV_DIM - SSD_WIDTH, NB), lambda h: (0, 0, 0)),
            pl.BlockSpec((NS, 1, 1, NB), lambda h: (0, h, 0, 0)),
            pl.BlockSpec((1, 1, NB), lambda h: (h, 0, 0)),
            pl.BlockSpec((1, 1, NB), lambda h: (h, 0, 0)),
            pl.BlockSpec((1, SSD_HEAD_DIM, SSD_STATE, NB), lambda h: (h, 0, 0, 0)),
        ],
        out_specs=(
            pl.BlockSpec((NS, SSD_HEAD_DIM, NB), lambda h: (0, h, 0)),
            pl.BlockSpec((1, SSD_HEAD_DIM, SSD_STATE, NB), lambda h: (h, 0, 0, 0)),
        ),
        out_shape=(
            jax.ShapeDtypeStruct((NS, SSD_WIDTH, NB), F32),
            jax.ShapeDtypeStruct((SSD_HEADS, SSD_HEAD_DIM, SSD_STATE, NB), F32),
        ),
        compiler_params=pltpu.CompilerParams(dimension_semantics=("arbitrary",),
                                             vmem_limit_bytes=VMEM_LIMIT),
        name="ssd_sample",
    )(xst, bct, dtt, alog_b, dsk_b, state_bm)


def _mix_s3_kernel(x_ref, yt_ref, z_ref, ylru_ref, nssd_ref, wout_ref, h1_ref):
    for t in range(NS):
        rs = slice(t * NB, (t + 1) * NB)
        h1_ref[rs, :] = _mixer_out(x_ref[rs, :], yt_ref[t].T, z_ref[rs, :], ylru_ref[rs, :],
                                   nssd_ref, wout_ref)


def _mix_s3(x_tm, yt, z, ylru, w):
    return pl.pallas_call(
        _mix_s3_kernel,
        out_shape=jax.ShapeDtypeStruct((NS * NB, D_MODEL), F32),
        compiler_params=pltpu.CompilerParams(vmem_limit_bytes=VMEM_LIMIT),
        name="mix_sample_out",
    )(x_tm, yt, z, ylru, w["norm_ssd"], w["w_out"])


def _pad_lanes(v):
    return jnp.pad(v.reshape(1, -1).astype(F32), ((0, 0), (0, LANES - v.size)))


def _block_diag(wb):
    k, n, _ = wb.shape
    eye = jnp.eye(k, dtype=wb.dtype)
    return (eye[:, None, :, None] * wb[:, :, None, :]).reshape(k * n, k * n)


def _prep_mixer_weights(a):
    w_in = a["w_in"][0]
    s1 = SSD_WIDTH
    s2 = s1 + SSD_CONV_DIM
    s3 = s2 + SSD_HEADS
    s4 = s3 + LRU_WIDTH
    pad = jnp.zeros((D_MODEL, LANES - SSD_HEADS), w_in.dtype)
    w_all = jnp.concatenate([w_in[:, :s2], w_in[:, s2:s3], pad, w_in[:, s3:]], axis=1).astype(BF16)
    hd = jnp.arange(SSD_WIDTH) // SSD_HEAD_DIM
    heads = jnp.arange(LANES)
    exp16 = (heads[:, None] == hd[None, :]).astype(BF16)
    expall = (heads[:, None] == (jnp.arange(SSD_HEADS * LANES) // LANES)[None, :]).astype(BF16)
    grp_rows = jnp.arange(SSD_GROUPS * SSD_STATE) // SSD_STATE
    grp_cols = hd // (SSD_HEADS // SSD_GROUPS)
    bdmask = (grp_rows[:, None] == grp_cols[None, :]).astype(F32)
    tri = (jnp.arange(CHUNK)[:, None] >= jnp.arange(CHUNK)[None, :]).astype(BF16)
    return {
        "norm_mix": _row(a["norm_mix"][0]),
        "w_all": w_all,
        "conv_ssd_w": a["conv_ssd_w"][0].astype(F32),
        "conv_ssd_b": _row(a["conv_ssd_b"][0]),
        "dt_bias": _pad_lanes(a["dt_bias"][0]),
        "a_log": _pad_lanes(a["a_log"][0]),
        "d_skip": _row(jnp.repeat(a["d_skip"][0], SSD_HEAD_DIM)),
        "norm_ssd": _row(a["norm_ssd"][0]),
        "conv_lru_w": a["conv_lru_w"][0].astype(F32),
        "conv_lru_b": _row(a["conv_lru_b"][0]),
        "w_r": _block_diag(a["w_rgate"][0]).astype(BF16),
        "b_r": _row(a["b_rgate"][0]),
        "w_i": _block_diag(a["w_igate"][0]).astype(BF16),
        "b_i": _row(a["b_igate"][0]),
        "lam": _row(a["lru_lambda"][0]),
        "norm_lru": _row(a["norm_lru"][0]),
        "w_out": a["w_out"][0].astype(BF16),
        "tri": tri, "exp16": exp16, "expall": expall, "bdmask": bdmask,
        "a_log_b": jnp.broadcast_to(a["a_log"][0].astype(F32)[:, None, None], (SSD_HEADS, 1, NB)),
        "d_skip_b": jnp.broadcast_to(a["d_skip"][0].astype(F32)[:, None, None], (SSD_HEADS, 1, NB)),
    }


def _time_major(v):
    return jnp.swapaxes(v, 0, 1).reshape(-1, v.shape[-1])


def _seq_major(v, n):
    return jnp.swapaxes(v.reshape(n, NB, v.shape[-1]), 0, 1)


def _layer_prompt(x, p, wm, wp):
    b = x.shape[0]
    h1, st, sconv, lstate, lconv = _mix_prompt(x, wm)
    y = _peer_ple(h1, p.reshape(-1, PLE_DIM), wp).reshape(x.shape)
    st5 = st.reshape(b, SSD_GROUPS, SSD_STATE, SSD_HEADS, SSD_HEAD_DIM)
    hpg = SSD_HEADS // SSD_GROUPS
    ssd_state = jnp.stack([st5[:, h // hpg, :, h, :] for h in range(SSD_HEADS)], axis=1)
    ssd_state = jnp.swapaxes(ssd_state, 2, 3)
    keep = slice(HIST - (CONV_W - 1), HIST)
    return y, ssd_state, sconv[:, keep, :], lstate[:, 0, :], lconv[:, keep, :]


def _layer_sample(x, p, s_ssd, s_sconv, s_lru, s_lconv, wm, wp):
    x_tm = _time_major(x)
    z, xst, bct, dtt, ylru, sconv, lstate, lconv = _mix_s1(
        x_tm, _time_major(s_sconv), _time_major(s_lconv), s_lru, wm)
    dtt = dtt[:, :SSD_HEADS, :].reshape(NS, SSD_HEADS, 1, NB)
    state_bm = jnp.transpose(s_ssd, (1, 2, 3, 0))
    yt, new_bm = _ssd_sample(xst, bct, dtt, wm["a_log_b"], wm["d_skip_b"], state_bm)
    h1 = _mix_s3(x_tm, yt, z, ylru, wm)
    y = _seq_major(_peer_ple(h1, _time_major(p), wp), NS)
    ssd_state = jnp.transpose(new_bm, (3, 0, 1, 2))
    return y, ssd_state, _seq_major(sconv, CONV_W - 1), lstate, _seq_major(lconv, CONV_W - 1)


def kernel(x_prompt, x_sample, state_ssd, state_ssd_conv, state_lru, state_lru_conv, p_prompt, p_sample, norm_mix, w_in, conv_ssd_w, conv_ssd_b, dt_bias, a_log, d_skip, norm_ssd, conv_lru_w, conv_lru_b, w_rgate, b_rgate, w_igate, b_igate, lru_lambda, norm_lru, w_out, norm_ffn, peer_wq, peer_keys, peer_u, peer_v, norm_ple_gate, w_ple_gate, w_ple_proj, norm_ple_out, norm_final):
    assert x_sample.shape[:2] == (NB, NS) and norm_mix.shape[0] == 1
    a = dict(norm_mix=norm_mix, w_in=w_in, conv_ssd_w=conv_ssd_w, conv_ssd_b=conv_ssd_b, dt_bias=dt_bias,
             a_log=a_log, d_skip=d_skip, norm_ssd=norm_ssd, conv_lru_w=conv_lru_w, conv_lru_b=conv_lru_b,
             w_rgate=w_rgate, b_rgate=b_rgate, w_igate=w_igate, b_igate=b_igate, lru_lambda=lru_lambda,
             norm_lru=norm_lru, w_out=w_out, norm_ffn=norm_ffn, peer_wq=peer_wq, peer_keys=peer_keys,
             peer_u=peer_u, peer_v=peer_v, norm_ple_gate=norm_ple_gate, w_ple_gate=w_ple_gate,
             w_ple_proj=w_ple_proj, norm_ple_out=norm_ple_out, norm_final=norm_final)
    wm = _prep_mixer_weights(a)
    wp = _prep_peer_weights(a)
    yp, p_ssd, p_sconv, p_lru, p_lconv = _layer_prompt(x_prompt, p_prompt[0], wm, wp)
    ys, s_ssd, s_sconv, s_lru, s_lconv = _layer_sample(
        x_sample, p_sample[0], state_ssd[0], state_ssd_conv[0], state_lru[0], state_lru_conv[0], wm, wp)
    return (yp, ys, p_ssd[None], p_sconv[None], p_lru[None], p_lconv[None],
            s_ssd[None], s_sconv[None], s_lru[None], s_lconv[None])
```

```python
import functools
import math

import jax
import jax.numpy as jnp
from jax import lax
from jax.experimental import pallas as pl
from jax.experimental.pallas import tpu as pltpu

F32 = jnp.float32
BF16 = jnp.bfloat16

D_MODEL = 1024
SSD_WIDTH = 1024
SSD_HEADS = 16
SSD_HEAD_DIM = 64
SSD_STATE = 64
SSD_GROUPS = 2
SSD_CONV_DIM = 1280
LRU_WIDTH = 1024
LRU_BLOCKS = 16
LRU_C = 8.0
CONV_W = 4
PEER_HEADS = 8
PEER_KEYS = 128
PEER_EXPERTS = PEER_KEYS * PEER_KEYS
PEER_TOPK = 16
PLE_DIM = 256
EPS = 1e-6

LANES = 128
SUBLANES = 8
VMEM_LIMIT = 56 * 1024 * 1024

P_Z = 0
P_XBC = P_Z + SSD_WIDTH
P_DT = P_XBC + SSD_CONV_DIM
P_GATE = P_DT + LANES
P_XL = P_GATE + LRU_WIDTH
P_ALL = P_XL + LRU_WIDTH


def _rms(x, g):
    return x * lax.rsqrt(jnp.mean(x * x, axis=-1, keepdims=True) + EPS) * g


def _sigmoid(x):
    return 1.0 / (1.0 + jnp.exp(-x))


def _silu(x):
    return x * _sigmoid(x)


def _gelu(x):
    c = math.sqrt(2.0 / math.pi)
    return 0.5 * x * (1.0 + jnp.tanh(c * (x + 0.044715 * (x * x * x))))


def _softplus(x):
    return jnp.maximum(x, 0.0) + jnp.log1p(jnp.exp(-jnp.abs(x)))


def _expm1(x):
    u = jnp.exp(x)
    near = (u - 1.0) * x / jnp.log(u)
    return jnp.where(u == 1.0, x, jnp.where(x < -1.0, u - 1.0, near))


def _dot(a, b):
    return jnp.dot(a, b, preferred_element_type=F32)


def _dot_nt(a, b):
    return lax.dot_general(a, b, (((1,), (1,)), ((), ())), preferred_element_type=F32)


def _split3(x):
    hi = x.astype(BF16)
    r = x - hi.astype(F32)
    mid = r.astype(BF16)
    lo = (r - mid.astype(F32)).astype(BF16)
    return hi, mid, lo


def _dot3(x, m):
    hi, mid, lo = _split3(x)
    return _dot(hi, m) + _dot(mid, m) + _dot(lo, m)


def _dot3_left(m, x):
    hi, mid, lo = _split3(x)
    return _dot(m, hi) + _dot(m, mid) + _dot(m, lo)


def _const_spec(shape):
    nd = len(shape)
    return pl.BlockSpec(shape, lambda *_: (0,) * nd, pipeline_mode=pl.Buffered(1))


PEER_TT = 512
PEER_EC = 1024
N_EC = PEER_EXPERTS // PEER_EC
I_PER_EC = PEER_EC // PEER_KEYS


def _peer_topk_tile(s1_ref, s2_ref, r2_ref, bb_ref, n1w_ref, aw_ref, t1_ref, t2_ref, lane0):
    ninf = -jnp.inf
    ls = pl.ds(lane0, LANES)
    for h in range(PEER_HEADS):
        rs = slice(h * PEER_KEYS, (h + 1) * PEER_KEYS)
        ranks = []
        for (s_ref, t_ref) in ((s1_ref, t1_ref), (s2_ref, t2_ref)):
            s = s_ref[rs, ls]
            rk = jnp.full((PEER_KEYS, LANES), float(PEER_TOPK), F32)
            for r in range(PEER_TOPK):
                m = jnp.max(s, axis=0, keepdims=True)
                t_ref[r:r + 1, :] = m
                hit = s == m
                rk = jnp.where(hit, float(r), rk)
                s = jnp.where(hit, ninf, s)
            ranks.append(rk)
        t1 = t1_ref[...]
        t2 = t2_ref[...]
        parts = [t1[0:1, :] + t2]
        for k1 in range(1, 8):
            parts.append(t1[k1:k1 + 1, :] + t2[0:8, :])
        parts.append(t1[8:16, :] + t2[0:1, :])
        cand = jnp.concatenate(parts, axis=0)
        top = t1[0:1, :] + t2[0:1, :]
        rank = jnp.zeros((1, LANES), F32)
        z = jnp.zeros((1, LANES), F32)
        thr = top
        for r in range(PEER_TOPK):
            m = jnp.max(cand, axis=0, keepdims=True)
            hit = cand == m
            cnt = jnp.sum(jnp.where(hit, 1.0, 0.0), axis=0, keepdims=True)
            live = rank < float(PEER_TOPK)
            z = z + jnp.where(live, cnt * jnp.exp(m - top), 0.0)
            thr = jnp.where(live, m, thr)
            rank = rank + cnt
            cand = jnp.where(hit, ninf, cand)
        n1 = jnp.zeros((PEER_TOPK, LANES), F32)
        for r2 in range(PEER_TOPK):
            n1 = n1 + jnp.where(t1 + t2[r2:r2 + 1, :] >= thr, 1.0, 0.0)
        n1_i = jnp.zeros((PEER_KEYS, LANES), F32)
        for r1 in range(PEER_TOPK):
            n1_i = jnp.where(ranks[0] == float(r1), n1[r1:r1 + 1, :], n1_i)
        a = jnp.exp(s1_ref[rs, ls] - t1[0:1, :]) * (1.0 / z)
        r2_ref[rs, ls] = ranks[1].astype(BF16)
        bb_ref[rs, ls] = jnp.exp(s2_ref[rs, ls] - t2[0:1, :]).astype(BF16)
        for c in range(N_EC):
            src = slice(c * I_PER_EC, (c + 1) * I_PER_EC)
            dst = slice((c * PEER_HEADS + h) * I_PER_EC, (c * PEER_HEADS + h + 1) * I_PER_EC)
            n1w_ref[dst, ls] = n1_i[src, :]
            aw_ref[dst, ls] = a[src, :]


def _peer_kernel(h1_ref, p_ref, nffn_ref, wq_ref, keys_ref, u_ref, vt_ref,
                 npg_ref, wpg_ref, wpp_ref, npo_ref, nfin_ref,
                 y_ref,
                 xn_ref, r2_ref, bb_ref, n1w_ref, aw_ref, t1_ref, t2_ref,
                 hta_ref, htb_ref, wg_ref, acc_ref):
    e = pl.program_id(1)
    tt = xn_ref.shape[0]

    def activations(dst_ref):
        dst_ref[...] = _dot_nt(u_ref[...], xn_ref[...])

    def gate_rows(src_ref, c, lane0):
        ls = slice(lane0, lane0 + LANES)
        rows_n, rows_a = [], []
        for h in range(PEER_HEADS):
            base = pl.multiple_of((c * PEER_HEADS + h) * I_PER_EC, I_PER_EC)
            rows_n.append(n1w_ref[pl.ds(base, I_PER_EC), ls])
            rows_a.append(aw_ref[pl.ds(base, I_PER_EC), ls])
        for il in range(I_PER_EC):
            es = slice(il * PEER_KEYS, (il + 1) * PEER_KEYS)
            w = None
            for h in range(PEER_HEADS):
                rs = slice(h * PEER_KEYS, (h + 1) * PEER_KEYS)
                n_b = jnp.broadcast_to(rows_n[h][il:il + 1, :], (PEER_KEYS, LANES)).astype(BF16)
                a_b = jnp.broadcast_to(rows_a[h][il:il + 1, :], (PEER_KEYS, LANES)).astype(BF16)
                term = jnp.where(r2_ref[rs, ls] < n_b, bb_ref[rs, ls], jnp.zeros((), BF16)) * a_b
                w = term if w is None else w + term
            g = _gelu(src_ref[es, ls]).astype(BF16)
            wg_ref[es, ls] = w * g

    def gate_and_project(src_ref, c):
        half = 2 * LANES
        for hs in range(tt // half):
            for lt in range(2):
                gate_rows(src_ref, c, hs * half + lt * LANES)
            cols = slice(hs * half, (hs + 1) * half)
            acc_ref[:, cols] += _dot(vt_ref[...], wg_ref[:, cols])

    @pl.when(e == 0)
    def _():
        xn = _rms(h1_ref[...], nffn_ref[...]).astype(BF16)
        xn_ref[...] = xn
        s1_ref, s2_ref = htb_ref, hta_ref
        for h in range(PEER_HEADS):
            q = _dot(xn, wq_ref[:, 2 * h * LANES:2 * (h + 1) * LANES]).astype(BF16)
            for c, s_ref in ((0, s1_ref), (1, s2_ref)):
                s_ref[h * PEER_KEYS:(h + 1) * PEER_KEYS, :] = _dot_nt(
                    keys_ref[2 * h + c], q[:, c * LANES:(c + 1) * LANES])

        def tile(j, carry):
            _peer_topk_tile(s1_ref, s2_ref, r2_ref, bb_ref, n1w_ref, aw_ref, t1_ref, t2_ref,
                            pl.multiple_of(j * LANES, LANES))
            return carry
        lax.fori_loop(0, tt // LANES, tile, 0)
        acc_ref[...] = jnp.zeros_like(acc_ref)
        activations(hta_ref)

    mid = jnp.logical_and(e >= 1, e <= N_EC - 1)

    @pl.when(jnp.logical_and(mid, e % 2 == 1))
    def _():
        activations(htb_ref)
        gate_and_project(hta_ref, e - 1)

    @pl.when(jnp.logical_and(mid, e % 2 == 0))
    def _():
        activations(hta_ref)
        gate_and_project(htb_ref, e - 1)

    @pl.when(e == N_EC)
    def _():
        gate_and_project(htb_ref if (N_EC - 1) % 2 else hta_ref, N_EC - 1)
        h2 = h1_ref[...] + acc_ref[...].T
        g = _sigmoid(_dot(_rms(h2, npg_ref[...]).astype(BF16), wpg_ref[...]))
        pe = _dot(p_ref[...].astype(BF16), wpp_ref[...])
        h3 = h2 + _rms(pe * g, npo_ref[...])
        y_ref[...] = _rms(h3, nfin_ref[...])


def _peer_ple(h1, p, w):
    t = h1.shape[0]
    tt = PEER_TT
    assert t % tt == 0
    grid = (t // tt, N_EC + 1)
    rows = PEER_HEADS * PEER_KEYS
    return pl.pallas_call(
        _peer_kernel,
        grid=grid,
        in_specs=[
            pl.BlockSpec((tt, D_MODEL), lambda i, e: (i, 0), pipeline_mode=pl.Buffered(1)),
            pl.BlockSpec((tt, PLE_DIM), lambda i, e: (i, 0), pipeline_mode=pl.Buffered(1)),
            _const_spec((1, D_MODEL)),
            _const_spec((D_MODEL, 2 * rows)),
            _const_spec((2 * PEER_HEADS, PEER_KEYS, LANES)),
            pl.BlockSpec((PEER_EC, D_MODEL), lambda i, e: (jnp.minimum(e, N_EC - 1), 0)),
            pl.BlockSpec((D_MODEL, PEER_EC), lambda i, e: (0, jnp.maximum(e - 1, 0))),
            _const_spec((1, D_MODEL)),
            _const_spec((D_MODEL, D_MODEL)),
            _const_spec((PLE_DIM, D_MODEL)),
            _const_spec((1, D_MODEL)),
            _const_spec((1, D_MODEL)),
        ],
        out_specs=pl.BlockSpec((tt, D_MODEL), lambda i, e: (i, 0)),
        out_shape=jax.ShapeDtypeStruct((t, D_MODEL), F32),
        scratch_shapes=[
            pltpu.VMEM((tt, D_MODEL), BF16),
            pltpu.VMEM((rows, tt), BF16),
            pltpu.VMEM((rows, tt), BF16),
            pltpu.VMEM((rows, tt), F32),
            pltpu.VMEM((rows, tt), F32),
            pltpu.VMEM((PEER_TOPK, LANES), F32),
            pltpu.VMEM((PEER_TOPK, LANES), F32),
            pltpu.VMEM((PEER_EC, tt), F32),
            pltpu.VMEM((PEER_EC, tt), F32),
            pltpu.VMEM((PEER_EC, tt), BF16),
            pltpu.VMEM((D_MODEL, tt), F32),
        ],
        compiler_params=pltpu.CompilerParams(
            dimension_semantics=("arbitrary", "arbitrary"),
            vmem_limit_bytes=VMEM_LIMIT),
        name="peer_ple",
    )(h1, p, w["norm_ffn"], w["wq"], w["keys"], w["u"], w["vt"],
      w["norm_ple_gate"], w["w_ple_gate"], w["w_ple_proj"], w["norm_ple_out"], w["norm_final"])


def _row(v):
    return v.reshape(1, -1).astype(F32)


def _prep_peer_weights(a):
    return {
        "norm_ffn": _row(a["norm_ffn"][0]),
        "wq": a["peer_wq"][0].astype(BF16),
        "keys": a["peer_keys"][0].reshape(2 * PEER_HEADS, PEER_KEYS, LANES).astype(BF16),
        "u": a["peer_u"][0].astype(BF16),
        "vt": a["peer_v"][0].astype(BF16).T,
        "norm_ple_gate": _row(a["norm_ple_gate"][0]),
        "w_ple_gate": a["w_ple_gate"][0].astype(BF16),
        "w_ple_proj": a["w_ple_proj"][0].astype(BF16),
        "norm_ple_out": _row(a["norm_ple_out"][0]),
        "norm_final": _row(a["norm_final"]),
    }


CHUNK = 128
MIX_TM = 256
HIST = SUBLANES


def _conv_from_ext(ext_ref, w, b, n):
    y = b + ext_ref[HIST - 3:HIST - 3 + n, :] * w[0:1, :]
    y = y + ext_ref[HIST - 2:HIST - 2 + n, :] * w[1:2, :]
    y = y + ext_ref[HIST - 1:HIST - 1 + n, :] * w[2:3, :]
    return y + ext_ref[HIST:HIST + n, :] * w[3:4, :]


def _lru_coeffs(xl, gates):
    wr_ref, br_ref, wi_ref, bi_ref, lam_ref = gates
    xlb = xl.astype(BF16)
    r = _sigmoid(_dot(xlb, wr_ref[...]) + br_ref[...])
    ig = _sigmoid(_dot(xlb, wi_ref[...]) + bi_ref[...])
    log_a = -LRU_C * r * _softplus(-lam_ref[...])
    a_t = jnp.exp(log_a)
    mult = jnp.sqrt(-_expm1(2.0 * log_a))
    return a_t, mult * (ig * xl)


def _ssd_chunk(xbc, dt, a, st_ref, consts, dsk):
    tri_ref, exp16_ref, expall_ref, bdmask_ref = consts
    tri = tri_ref[...]
    da = dt * a
    cs = _dot3_left(tri, da)
    cs_e = _dot3(cs, exp16_ref[...])
    dt_e = _dot3(dt, exp16_ref[...])
    xs = xbc[:, 0:SSD_WIDTH]
    xdt = xs * dt_e
    bmat = xbc[:, SSD_WIDTH:SSD_WIDTH + LANES]
    cmat = xbc[:, SSD_WIDTH + LANES:SSD_WIDTH + 2 * LANES]
    last_e = cs_e[CHUNK - 1:CHUNK, :]
    xd = (xdt * jnp.exp(last_e - cs_e)).astype(BF16)
    st = st_ref[...]
    y_off = _dot(cmat.astype(BF16), st.astype(BF16)) * jnp.exp(cs_e)
    st_ref[...] = st * jnp.exp(last_e) + _dot(bmat.T.astype(BF16), xd) * bdmask_ref[...]

    col_b = _dot3(cs, expall_ref[...])
    cs_t = cs.T
    lane = lax.broadcasted_iota(jnp.int32, (CHUNK, LANES), 1)
    row = lax.broadcasted_iota(jnp.int32, (CHUNK, LANES), 0)
    causal = row >= lane
    low = lane < SSD_STATE
    bmat_b = bmat.astype(BF16)
    cb = [_dot_nt(jnp.where(low, cmat, 0.0).astype(BF16), bmat_b),
          _dot_nt(jnp.where(low, 0.0, cmat).astype(BF16), bmat_b)]
    xdt_b = xdt.astype(BF16)
    pieces = []
    for hp in range(SSD_HEADS // 2):
        cols = slice(hp * LANES, (hp + 1) * LANES)
        outs = []
        for h in (2 * hp, 2 * hp + 1):
            seg = jnp.exp(col_b[:, h * LANES:(h + 1) * LANES] - cs_t[h:h + 1, :])
            m = (cb[h // (SSD_HEADS // SSD_GROUPS)] * jnp.where(causal, seg, 0.0)).astype(BF16)
            outs.append(_dot(m, xdt_b[:, cols]))
        pieces.append(jnp.where(low, outs[0], outs[1]))
    y_diag = jnp.concatenate(pieces, axis=1)
    return y_diag + y_off + dsk * xs


def _mixer_out(x, y, z, y_lru, nssd_ref, wout_ref):
    y_ssd = _rms(y * _silu(z), nssd_ref[...])
    return (x + _dot(y_ssd.astype(BF16), wout_ref[0:SSD_WIDTH, :])
            + _dot(y_lru.astype(BF16), wout_ref[SSD_WIDTH:SSD_WIDTH + LRU_WIDTH, :]))


def _mix_p_kernel(x_ref, nmix_ref, wall_ref, cws_ref, cbs_ref, dtb_ref, alog_ref, dsk_ref, nssd_ref,
                  cwl_ref, cbl_ref, wr_ref, br_ref, wi_ref, bi_ref, lam_ref, nlru_ref, wout_ref,
                  tri_ref, exp16_ref, expall_ref, bdmask_ref,
                  h1_ref, st_out_ref, sconv_out_ref, lstate_out_ref, lconv_out_ref,
                  exts_ref, extl_ref, st_ref, hcar_ref, y_ref, pa_ref, hb_ref):
    l = pl.program_id(1)
    tm = x_ref.shape[0]

    @pl.when(l == 0)
    def _():
        exts_ref[0:HIST, :] = jnp.zeros((HIST, SSD_CONV_DIM), F32)
        extl_ref[0:HIST, :] = jnp.zeros((HIST, LRU_WIDTH), F32)
        st_ref[...] = jnp.zeros_like(st_ref)
        hcar_ref[...] = jnp.zeros_like(hcar_ref)

    x = x_ref[...]
    xn = _rms(x, nmix_ref[...]).astype(BF16)
    proj = _dot(xn, wall_ref[...])
    z = proj[:, P_Z:P_XBC]
    gate = proj[:, P_GATE:P_XL]

    exts_ref[HIST:HIST + tm, :] = proj[:, P_XBC:P_DT]
    xbc = _silu(_conv_from_ext(exts_ref, cws_ref[...], cbs_ref[...], tm))
    exts_ref[0:HIST, :] = exts_ref[tm:tm + HIST, :]
    dt = _softplus(proj[:, P_DT:P_GATE] + dtb_ref[...])
    a = -jnp.exp(alog_ref[...])
    consts = (tri_ref, exp16_ref, expall_ref, bdmask_ref)
    for c in range(tm // CHUNK):
        rs = slice(c * CHUNK, (c + 1) * CHUNK)
        y_ref[rs, :] = _ssd_chunk(xbc[rs, :], dt[rs, :], a, st_ref, consts, dsk_ref[...])

    extl_ref[HIST:HIST + tm, :] = proj[:, P_XL:P_ALL]
    xl = _conv_from_ext(extl_ref, cwl_ref[...], cbl_ref[...], tm)
    extl_ref[0:HIST, :] = extl_ref[tm:tm + HIST, :]
    a_t, bx = _lru_coeffs(xl, (wr_ref, br_ref, wi_ref, bi_ref, lam_ref))
    seg = tm // SUBLANES
    hs_cols = []
    for j in range(LRU_WIDTH // LANES):
        cols = slice(j * LANES, (j + 1) * LANES)
        pa_ref[j] = a_t[:, cols]
        hb_ref[j] = bx[:, cols]
        hloc = jnp.zeros((SUBLANES, LANES), F32)
        ploc = jnp.ones((SUBLANES, LANES), F32)
        for tau in range(seg):
            rows = pl.ds(tau, SUBLANES, stride=seg)
            a_r = pa_ref[j, rows, :]
            hloc = a_r * hloc + hb_ref[j, rows, :]
            ploc = a_r * ploc
            hb_ref[j, rows, :] = hloc
            pa_ref[j, rows, :] = ploc
        carry = hcar_ref[0:1, cols]
        parts = []
        for s in range(SUBLANES):
            rs = slice(s * seg, (s + 1) * seg)
            parts.append(hb_ref[j, rs, :] + pa_ref[j, rs, :] * carry)
            carry = hloc[s:s + 1, :] + ploc[s:s + 1, :] * carry
        hcar_ref[0:1, cols] = carry
        hs_cols.append(jnp.concatenate(parts, axis=0))
    hs = jnp.concatenate(hs_cols, axis=1)
    y_lru = _rms(hs * _gelu(gate), nlru_ref[...])

    h1_ref[...] = _mixer_out(x, y_ref[...], z, y_lru, nssd_ref, wout_ref)

    @pl.when(l == pl.num_programs(1) - 1)
    def _():
        st_out_ref[0] = st_ref[...]
        sconv_out_ref[0] = exts_ref[0:HIST, :]
        lstate_out_ref[0] = hcar_ref[...]
        lconv_out_ref[0] = extl_ref[0:HIST, :]


def _mix_prompt(x, w):
    b, l, _ = x.shape
    tm = MIX_TM
    assert l % tm == 0
    nl = l // tm
    names = ["norm_mix", "w_all", "conv_ssd_w", "conv_ssd_b", "dt_bias", "a_log", "d_skip", "norm_ssd",
             "conv_lru_w", "conv_lru_b", "w_r", "b_r", "w_i", "b_i", "lam", "norm_lru", "w_out",
             "tri", "exp16", "expall", "bdmask"]
    ws = [w[n] for n in names]
    gs = SSD_GROUPS * SSD_STATE
    out_shape = (
        jax.ShapeDtypeStruct((b * l, D_MODEL), F32),
        jax.ShapeDtypeStruct((b, gs, SSD_WIDTH), F32),
        jax.ShapeDtypeStruct((b, HIST, SSD_CONV_DIM), F32),
        jax.ShapeDtypeStruct((b, SUBLANES, LRU_WIDTH), F32),
        jax.ShapeDtypeStruct((b, HIST, LRU_WIDTH), F32),
    )
    per_b = lambda shape: pl.BlockSpec((1,) + shape, lambda i, j: (i, 0, 0))
    return pl.pallas_call(
        _mix_p_kernel,
        grid=(b, nl),
        in_specs=[pl.BlockSpec((tm, D_MODEL), lambda i, j: (i * nl + j, 0))]
        + [_const_spec(v.shape) for v in ws],
        out_specs=(
            pl.BlockSpec((tm, D_MODEL), lambda i, j: (i * nl + j, 0)),
            per_b((gs, SSD_WIDTH)), per_b((HIST, SSD_CONV_DIM)),
            per_b((SUBLANES, LRU_WIDTH)), per_b((HIST, LRU_WIDTH)),
        ),
        out_shape=out_shape,
        scratch_shapes=[
            pltpu.VMEM((HIST + tm, SSD_CONV_DIM), F32),
            pltpu.VMEM((HIST + tm, LRU_WIDTH), F32),
            pltpu.VMEM((gs, SSD_WIDTH), F32),
            pltpu.VMEM((SUBLANES, LRU_WIDTH), F32),
            pltpu.VMEM((tm, SSD_WIDTH), F32),
            pltpu.VMEM((LRU_WIDTH // LANES, tm, LANES), F32),
            pltpu.VMEM((LRU_WIDTH // LANES, tm, LANES), F32),
        ],
        compiler_params=pltpu.CompilerParams(
            dimension_semantics=("arbitrary", "arbitrary"),
            vmem_limit_bytes=VMEM_LIMIT),
        name="mix_prompt",
    )(x.reshape(b * l, D_MODEL), *ws)


NS = 4
NB = 128


def _mix_s1_kernel(x_ref, bufs_ref, bufl_ref, lst_ref, nmix_ref, wall_ref, cws_ref, cbs_ref, dtb_ref,
                   cwl_ref, cbl_ref, wr_ref, br_ref, wi_ref, bi_ref, lam_ref, nlru_ref,
                   z_ref, xst_ref, bct_ref, dtt_ref, ylru_ref, sconv_ref, lstate_ref, lconv_ref):
    xn = _rms(x_ref[...], nmix_ref[...]).astype(BF16)
    proj = _dot(xn, wall_ref[...])
    z_ref[...] = proj[:, P_Z:P_XBC]

    def blocks(buf_ref, raw):
        return ([buf_ref[j * NB:(j + 1) * NB, :] for j in range(CONV_W - 1)]
                + [raw[t * NB:(t + 1) * NB, :] for t in range(NS)])

    def conv(xp, w, b, t):
        y = b + xp[t] * w[0:1, :]
        for k in range(1, CONV_W):
            y = y + xp[t + k] * w[k:k + 1, :]
        return y

    xp = blocks(bufs_ref, proj[:, P_XBC:P_DT])
    for j in range(CONV_W - 1):
        sconv_ref[j * NB:(j + 1) * NB, :] = xp[NS + j]
    for t in range(NS):
        xbc = _silu(conv(xp, cws_ref[...], cbs_ref[...], t))
        dt = _softplus(proj[t * NB:(t + 1) * NB, P_DT:P_GATE] + dtb_ref[...])
        xst_ref[t] = xbc[:, 0:SSD_WIDTH].T
        bct_ref[t] = xbc[:, SSD_WIDTH:SSD_CONV_DIM].T
        dtt_ref[t] = dt.T

    xp = blocks(bufl_ref, proj[:, P_XL:P_ALL])
    for j in range(CONV_W - 1):
        lconv_ref[j * NB:(j + 1) * NB, :] = xp[NS + j]
    h = lst_ref[...]
    for t in range(NS):
        rs = slice(t * NB, (t + 1) * NB)
        xl = conv(xp, cwl_ref[...], cbl_ref[...], t)
        a_t, bx = _lru_coeffs(xl, (wr_ref, br_ref, wi_ref, bi_ref, lam_ref))
        h = a_t * h + bx
        ylru_ref[rs, :] = _rms(h * _gelu(proj[rs, P_GATE:P_XL]), nlru_ref[...])
    lstate_ref[...] = h


def _mix_s1(x_tm, bufs_tm, bufl_tm, lstate, w):
    n = NS * NB
    names = ["norm_mix", "w_all", "conv_ssd_w", "conv_ssd_b", "dt_bias",
             "conv_lru_w", "conv_lru_b", "w_r", "b_r", "w_i", "b_i", "lam", "norm_lru"]
    out_shape = (
        jax.ShapeDtypeStruct((n, SSD_WIDTH), F32),
        jax.ShapeDtypeStruct((NS, SSD_WIDTH, NB), F32),
        jax.ShapeDtypeStruct((NS, SSD_CONV_DIM - SSD_WIDTH, NB), F32),
        jax.ShapeDtypeStruct((NS, LANES, NB), F32),
        jax.ShapeDtypeStruct((n, LRU_WIDTH), F32),
        jax.ShapeDtypeStruct(((CONV_W - 1) * NB, SSD_CONV_DIM), F32),
        jax.ShapeDtypeStruct((NB, LRU_WIDTH), F32),
        jax.ShapeDtypeStruct(((CONV_W - 1) * NB, LRU_WIDTH), F32),
    )
    return pl.pallas_call(
        _mix_s1_kernel, out_shape=out_shape,
        compiler_params=pltpu.CompilerParams(vmem_limit_bytes=VMEM_LIMIT),
        name="mix_sample_in",
    )(x_tm, bufs_tm, bufl_tm, lstate, *[w[k] for k in names])


def _ssd_s_kernel(xs_ref, bc_ref, dt_ref, alog_ref, dsk_ref, s_ref, y_ref, ns_ref):
    h = pl.program_id(0)
    g = h // (SSD_HEADS // SSD_GROUPS)
    b_rows = pl.ds(pl.multiple_of(g * SSD_STATE, SSD_STATE), SSD_STATE)
    c_rows = pl.ds(pl.multiple_of(LANES + g * SSD_STATE, SSD_STATE), SSD_STATE)
    a = -jnp.exp(alog_ref[0])
    ns_ref[...] = s_ref[...]
    for t in range(NS):
        dtr = dt_ref[t, 0]
        decay = jnp.exp(dtr * a)
        bt = bc_ref[t, b_rows, :]
        ct = bc_ref[t, c_rows, :]
        xs = xs_ref[t]
        xdt = xs * dtr
        for p in range(SSD_HEAD_DIM):
            s_p = decay * ns_ref[0, p] + xdt[p:p + 1, :] * bt
            ns_ref[0, p] = s_p
            y_ref[t, p:p + 1, :] = jnp.sum(ct * s_p, axis=0, keepdims=True)
        y_ref[t] = y_ref[t] + dsk_ref[0] * xs


def _ssd_sample(xst, bct, dtt, alog_b, dsk_b, state_bm):
    return pl.pallas_call(
        _ssd_s_kernel,
        grid=(SSD_HEADS,),
        in_specs=[
            pl.BlockSpec((NS, SSD_HEAD_DIM, NB), lambda h: (0, h, 0)),
            pl.BlockSpec((NS, SSD_CONV_DIM - SSD_WIDTH, NB), lambda h: (0, 0, 0)),
            pl.BlockSpec((NS, 1, 1, NB), lambda h: (0, h, 0, 0)),
            pl.BlockSpec((1, 1, NB), lambda h: (h, 0, 0)),
            pl.BlockSpec((1, 1, NB), lambda h: (h, 0, 0)),
            pl.BlockSpec((1, SSD_HEAD_DIM, SSD_STATE, NB), lambda h: (h, 0, 0, 0)),
        ],
        out_specs=(
            pl.BlockSpec((NS, SSD_HEAD_DIM, NB), lambda h: (0, h, 0)),
            pl.BlockSpec((1, SSD_HEAD_DIM, SSD_STATE, NB), lambda h: (h, 0, 0, 0)),
        ),
        out_shape=(
            jax.ShapeDtypeStruct((NS, SSD_WIDTH, NB), F32),
            jax.ShapeDtypeStruct((SSD_HEADS, SSD_HEAD_DIM, SSD_STATE, NB), F32),
        ),
        compiler_params=pltpu.CompilerParams(dimension_semantics=("arbitrary",),
                                             vmem_limit_bytes=VMEM_LIMIT),
        name="ssd_sample",
    )(xst, bct, dtt, alog_b, dsk_b, state_bm)


def _mix_s3_kernel(x_ref, yt_ref, z_ref, ylru_ref, nssd_ref, wout_ref, h1_ref):
    for t in range(NS):
        rs = slice(t * NB, (t + 1) * NB)
        h1_ref[rs, :] = _mixer_out(x_ref[rs, :], yt_ref[t].T, z_ref[rs, :], ylru_ref[rs, :],
                                   nssd_ref, wout_ref)


def _mix_s3(x_tm, yt, z, ylru, w):
    return pl.pallas_call(
        _mix_s3_kernel,
        out_shape=jax.ShapeDtypeStruct((NS * NB, D_MODEL), F32),
        compiler_params=pltpu.CompilerParams(vmem_limit_bytes=VMEM_LIMIT),
        name="mix_sample_out",
    )(x_tm, yt, z, ylru, w["norm_ssd"], w["w_out"])


def _pad_lanes(v):
    return jnp.pad(v.reshape(1, -1).astype(F32), ((0, 0), (0, LANES - v.size)))


def _block_diag(wb):
    k, n, _ = wb.shape
    eye = jnp.eye(k, dtype=wb.dtype)
    return (eye[:, None, :, None] * wb[:, :, None, :]).reshape(k * n, k * n)


def _prep_mixer_weights(a):
    w_in = a["w_in"][0]
    s1 = SSD_WIDTH
    s2 = s1 + SSD_CONV_DIM
    s3 = s2 + SSD_HEADS
    s4 = s3 + LRU_WIDTH
    pad = jnp.zeros((D_MODEL, LANES - SSD_HEADS), w_in.dtype)
    w_all = jnp.concatenate([w_in[:, :s2], w_in[:, s2:s3], pad, w_in[:, s3:]], axis=1).astype(BF16)
    hd = jnp.arange(SSD_WIDTH) // SSD_HEAD_DIM
    heads = jnp.arange(LANES)
    exp16 = (heads[:, None] == hd[None, :]).astype(BF16)
    expall = (heads[:, None] == (jnp.arange(SSD_HEADS * LANES) // LANES)[None, :]).astype(BF16)
    grp_rows = jnp.arange(SSD_GROUPS * SSD_STATE) // SSD_STATE
    grp_cols = hd // (SSD_HEADS // SSD_GROUPS)
    bdmask = (grp_rows[:, None] == grp_cols[None, :]).astype(F32)
    tri = (jnp.arange(CHUNK)[:, None] >= jnp.arange(CHUNK)[None, :]).astype(BF16)
    return {
        "norm_mix": _row(a["norm_mix"][0]),
        "w_all": w_all,
        "conv_ssd_w": a["conv_ssd_w"][0].astype(F32),
        "conv_ssd_b": _row(a["conv_ssd_b"][0]),
        "dt_bias": _pad_lanes(a["dt_bias"][0]),
        "a_log": _pad_lanes(a["a_log"][0]),
        "d_skip": _row(jnp.repeat(a["d_skip"][0], SSD_HEAD_DIM)),
        "norm_ssd": _row(a["norm_ssd"][0]),
        "conv_lru_w": a["conv_lru_w"][0].astype(F32),
        "conv_lru_b": _row(a["conv_lru_b"][0]),
        "w_r": _block_diag(a["w_rgate"][0]).astype(BF16),
        "b_r": _row(a["b_rgate"][0]),
        "w_i": _block_diag(a["w_igate"][0]).astype(BF16),
        "b_i": _row(a["b_igate"][0]),
        "lam": _row(a["lru_lambda"][0]),
        "norm_lru": _row(a["norm_lru"][0]),
        "w_out": a["w_out"][0].astype(BF16),
        "tri": tri, "exp16": exp16, "expall": expall, "bdmask": bdmask,
        "a_log_b": jnp.broadcast_to(a["a_log"][0].astype(F32)[:, None, None], (SSD_HEADS, 1, NB)),
        "d_skip_b": jnp.broadcast_to(a["d_skip"][0].astype(F32)[:, None, None], (SSD_HEADS, 1, NB)),
    }


def _time_major(v):
    return jnp.swapaxes(v, 0, 1).reshape(-1, v.shape[-1])


def _seq_major(v, n):
    return jnp.swapaxes(v.reshape(n, NB, v.shape[-1]), 0, 1)


def _layer_prompt(x, p, wm, wp):
    b = x.shape[0]
    h1, st, sconv, lstate, lconv = _mix_prompt(x, wm)
    y = _peer_ple(h1, p.reshape(-1, PLE_DIM), wp).reshape(x.shape)
    st5 = st.reshape(b, SSD_GROUPS, SSD_STATE, SSD_HEADS, SSD_HEAD_DIM)
    hpg = SSD_HEADS // SSD_GROUPS
    ssd_state = jnp.stack([st5[:, h // hpg, :, h, :] for h in range(SSD_HEADS)], axis=1)
    ssd_state = jnp.swapaxes(ssd_state, 2, 3)
    keep = slice(HIST - (CONV_W - 1), HIST)
    return y, ssd_state, sconv[:, keep, :], lstate[:, 0, :], lconv[:, keep, :]


def _layer_sample(x, p, s_ssd, s_sconv, s_lru, s_lconv, wm, wp):
    x_tm = _time_major(x)
    z, xst, bct, dtt, ylru, sconv, lstate, lconv = _mix_s1(
        x_tm, _time_major(s_sconv), _time_major(s_lconv), s_lru, wm)
    dtt = dtt[:, :SSD_HEADS, :].reshape(NS, SSD_HEADS, 1, NB)
    state_bm = jnp.transpose(s_ssd, (1, 2, 3, 0))
    yt, new_bm = _ssd_sample(xst, bct, dtt, wm["a_log_b"], wm["d_skip_b"], state_bm)
    h1 = _mix_s3(x_tm, yt, z, ylru, wm)
    y = _seq_major(_peer_ple(h1, _time_major(p), wp), NS)
    ssd_state = jnp.transpose(new_bm, (3, 0, 1, 2))
    return y, ssd_state, _seq_major(sconv, CONV_W - 1), lstate, _seq_major(lconv, CONV_W - 1)


def kernel(x_prompt, x_sample, state_ssd, state_ssd_conv, state_lru, state_lru_conv, p_prompt, p_sample, norm_mix, w_in, conv_ssd_w, conv_ssd_b, dt_bias, a_log, d_skip, norm_ssd, conv_lru_w, conv_lru_b, w_rgate, b_rgate, w_igate, b_igate, lru_lambda, norm_lru, w_out, norm_ffn, peer_wq, peer_keys, peer_u, peer_v, norm_ple_gate, w_ple_gate, w_ple_proj, norm_ple_out, norm_final):
    assert x_sample.shape[:2] == (NB, NS) and norm_mix.shape[0] == 1
    a = dict(norm_mix=norm_mix, w_in=w_in, conv_ssd_w=conv_ssd_w, conv_ssd_b=conv_ssd_b, dt_bias=dt_bias,
             a_log=a_log, d_skip=d_skip, norm_ssd=norm_ssd, conv_lru_w=conv_lru_w, conv_lru_b=conv_lru_b,
             w_rgate=w_rgate, b_rgate=b_rgate, w_igate=w_igate, b_igate=b_igate, lru_lambda=lru_lambda,
             norm_lru=norm_lru, w_out=w_out, norm_ffn=norm_ffn, peer_wq=peer_wq, peer_keys=peer_keys,
             peer_u=peer_u, peer_v=peer_v, norm_ple_gate=norm_ple_gate, w_ple_gate=w_ple_gate,
             w_ple_proj=w_ple_proj, norm_ple_out=norm_ple_out, norm_final=norm_final)
    wm = _prep_mixer_weights(a)
    wp = _prep_peer_weights(a)
    yp, p_ssd, p_sconv, p_lru, p_lconv = _layer_prompt(x_prompt, p_prompt[0], wm, wp)
    ys, s_ssd, s_sconv, s_lru, s_lconv = _layer_sample(
        x_sample, p_sample[0], state_ssd[0], state_ssd_conv[0], state_lru[0], state_lru_conv[0], wm, wp)
    return (yp, ys, p_ssd[None], p_sconv[None], p_lru[None], p_lconv[None],
            s_ssd[None], s_sconv[None], s_lru[None], s_lconv[None])
```

```python
import functools
import math

import jax
import jax.numpy as jnp
from jax import lax
from jax.experimental import pallas as pl
from jax.experimental.pallas import tpu as pltpu

F32 = jnp.float32
BF16 = jnp.bfloat16

D_MODEL = 1024
SSD_WIDTH = 1024
SSD_HEADS = 16
SSD_HEAD_DIM = 64
SSD_STATE = 64
SSD_GROUPS = 2
SSD_CONV_DIM = 1280
LRU_WIDTH = 1024
LRU_BLOCKS = 16
LRU_C = 8.0
CONV_W = 4
PEER_HEADS = 8
PEER_KEYS = 128
PEER_EXPERTS = PEER_KEYS * PEER_KEYS
PEER_TOPK = 16
PLE_DIM = 256
EPS = 1e-6

LANES = 128
SUBLANES = 8
MXU_DIM = 256
VMEM_LIMIT = 60 * 1024 * 1024

P_Z = 0
P_XBC = P_Z + SSD_WIDTH
P_DT = P_XBC + SSD_CONV_DIM
P_GATE = P_DT + LANES
P_XL = P_GATE + LRU_WIDTH
P_ALL = P_XL + LRU_WIDTH


def _rms(x, g):
    return x * lax.rsqrt(jnp.mean(x * x, axis=-1, keepdims=True) + EPS) * g


def _sigmoid(x):
    return 1.0 / (1.0 + jnp.exp(-x))


def _silu(x):
    return x * _sigmoid(x)


def _gelu(x):
    k1 = -2.0 * math.sqrt(2.0 / math.pi) * math.log2(math.e)
    k3 = k1 * 0.044715
    return x / (1.0 + jnp.exp2(x * (k1 + k3 * (x * x))))


def _softplus(x):
    return jnp.maximum(x, 0.0) + jnp.log1p(jnp.exp(-jnp.abs(x)))


def _expm1(x):
    u = jnp.exp(x)
    near = (u - 1.0) * x / jnp.log(u)
    return jnp.where(u == 1.0, x, jnp.where(x < -1.0, u - 1.0, near))


def _dot(a, b):
    return jnp.dot(a, b, preferred_element_type=F32)


def _dot_nt(a, b):
    return lax.dot_general(a, b, (((1,), (1,)), ((), ())), preferred_element_type=F32)


def _split3(x):
    hi = x.astype(BF16)
    r = x - hi.astype(F32)
    mid = r.astype(BF16)
    lo = (r - mid.astype(F32)).astype(BF16)
    return hi, mid, lo


def _dot3(x, m):
    hi, mid, lo = _split3(x)
    return _dot(hi, m) + _dot(mid, m) + _dot(lo, m)


def _dot3_left(m, x):
    hi, mid, lo = _split3(x)
    return _dot(m, hi) + _dot(m, mid) + _dot(m, lo)


def _const_spec(shape):
    nd = len(shape)
    return pl.BlockSpec(shape, lambda *_: (0,) * nd, pipeline_mode=pl.Buffered(1))


PEER_TT = 512
PEER_EC = 1024
N_EC = PEER_EXPERTS // PEER_EC
I_PER_EC = PEER_EC // PEER_KEYS


def _batcher_network(lo, hi):
    def merge(lo, hi, r):
        step = 2 * r
        if step < hi - lo:
            yield from merge(lo, hi, step)
            yield from merge(lo + r, hi, step)
            yield from ((i, i + r) for i in range(lo + r, hi - r, step))
        else:
            yield (lo, lo + r)
    if hi - lo >= 1:
        mid = lo + (hi - lo) // 2
        yield from _batcher_network(lo, mid)
        yield from _batcher_network(mid + 1, hi)
        yield from merge(lo, hi, 1)


SORT16 = tuple(_batcher_network(0, PEER_TOPK - 1))


def _cmpx(v, a, b):
    v[a], v[b] = jnp.maximum(v[a], v[b]), jnp.minimum(v[a], v[b])


def _top16_desc(tiles):
    v = list(tiles)
    for a, b in SORT16:
        _cmpx(v, a, b)
    for shift in (4, 2, 1):
        w = [pltpu.roll(x, shift, 0) for x in v]
        v = [jnp.maximum(v[k], w[PEER_TOPK - 1 - k]) for k in range(PEER_TOPK)]
        for d in (8, 4, 2, 1):
            for k in range(PEER_TOPK):
                if not k & d:
                    _cmpx(v, k, k + d)
    return v


def _peer_topk_tile(s1_ref, s2_ref, s2k_ref, bb_ref, tauw_ref, aw_ref, j):
    ninf = jnp.full((SUBLANES, LANES), -jnp.inf, F32)
    sub = lax.broadcasted_iota(jnp.int32, (SUBLANES, LANES), 0)

    def rows_of(vals):
        out = vals[SUBLANES - 1]
        for r in range(SUBLANES - 2, -1, -1):
            out = jnp.where(sub == r, vals[r], out)
        return out

    for h in range(PEER_HEADS):
        rs = slice(h * PEER_KEYS, (h + 1) * PEER_KEYS)
        s1 = s1_ref[j, rs, :]
        s2 = s2_ref[j, rs, :]
        t1 = _top16_desc([s1[k * SUBLANES:(k + 1) * SUBLANES, :] for k in range(PEER_TOPK)])
        t2 = _top16_desc([s2[k * SUBLANES:(k + 1) * SUBLANES, :] for k in range(PEER_TOPK)])
        t2_lo, t2_hi, t1_hi = rows_of(t2[0:8]), rows_of(t2[8:16]), rows_of(t1[8:16])
        cand = [t1[0] + t2_lo, t1[0] + t2_hi] + [t1[k1] + t2_lo for k1 in range(1, 8)] + [t1_hi + t2[0]]
        best = _top16_desc(cand + [ninf] * (PEER_TOPK - len(cand)))
        thr = best[PEER_TOPK - 1]
        z = jnp.zeros((SUBLANES, LANES), F32)
        for k in range(PEER_TOPK):
            z = z + jnp.exp(best[k] - best[0])
        inv_z = 1.0 / z
        tau = jnp.full((PEER_KEYS, LANES), jnp.inf, F32)
        for r2 in range(PEER_TOPK):
            t2_b = jnp.concatenate([t2[r2]] * PEER_TOPK, axis=0)
            tau = jnp.where(s1 + t2_b >= jnp.concatenate([thr] * PEER_TOPK, axis=0), t2_b, tau)
        a = jnp.exp(s1 - jnp.concatenate([t1[0]] * PEER_TOPK, axis=0)) * jnp.concatenate([inv_z] * PEER_TOPK, axis=0)
        s2k_ref[j, rs, :] = s2
        bb_ref[j, rs, :] = jnp.exp(s2 - jnp.concatenate([t2[0]] * PEER_TOPK, axis=0))
        for c in range(N_EC):
            src = slice(c * I_PER_EC, (c + 1) * I_PER_EC)
            dst = slice((c * PEER_HEADS + h) * I_PER_EC, (c * PEER_HEADS + h + 1) * I_PER_EC)
            tauw_ref[j, dst, :] = tau[src, :]
            aw_ref[j, dst, :] = a[src, :]


def _peer_kernel(h1_ref, p_ref, nffn_ref, wq_ref, keys_ref, u_ref, vt_ref,
                 npg_ref, wpg_ref, wpp_ref, npo_ref, nfin_ref,
                 y_ref,
                 xn_ref, s2k_ref, bb_ref, tauw_ref, aw_ref,
                 hta_ref, htb_ref, wg_ref, acc_ref):
    e = pl.program_id(1)
    tt = xn_ref.shape[0]

    def to_slabs(dst_ref, rows, val):
        for j in range(tt // LANES):
            dst_ref[j, rows, :] = val[:, j * LANES:(j + 1) * LANES]

    def activations(dst_ref):
        to_slabs(dst_ref, slice(None), _dot_nt(u_ref[...], xn_ref[...]))

    def gate_rows(src_ref, c, j):
        rows_t, rows_a = [], []
        for h in range(PEER_HEADS):
            base = pl.multiple_of((c * PEER_HEADS + h) * I_PER_EC, I_PER_EC)
            rows_t.append(tauw_ref[j, pl.ds(base, I_PER_EC), :])
            rows_a.append(aw_ref[j, pl.ds(base, I_PER_EC), :])
        for il in range(I_PER_EC):
            es = slice(il * PEER_KEYS, (il + 1) * PEER_KEYS)
            w = None
            for h in range(PEER_HEADS):
                rs = slice(h * PEER_KEYS, (h + 1) * PEER_KEYS)
                tau_b = jnp.broadcast_to(rows_t[h][il:il + 1, :], (PEER_KEYS, LANES))
                a_b = jnp.broadcast_to(rows_a[h][il:il + 1, :], (PEER_KEYS, LANES))
                term = jnp.where(s2k_ref[j, rs, :] >= tau_b, bb_ref[j, rs, :], 0.0) * a_b
                w = term if w is None else w + term
            wg_ref[j, es, :] = (w * _gelu(src_ref[j, es, :])).astype(BF16)

    def gate(src_ref, c):
        for j in range(tt // LANES):
            gate_rows(src_ref, c, j)

    def project():
        for hs in range(tt // (2 * LANES)):
            j0, j1 = 2 * hs, 2 * hs + 1
            out = _dot(vt_ref[...], jnp.concatenate([wg_ref[j0], wg_ref[j1]], axis=1))
            acc_ref[j0] += out[:, 0:LANES]
            acc_ref[j1] += out[:, LANES:2 * LANES]

    @pl.when(e == 0)
    def _():
        xn = _rms(h1_ref[...], nffn_ref[...]).astype(BF16)
        xn_ref[...] = xn
        s1_ref, s2_ref = htb_ref, hta_ref
        for h in range(PEER_HEADS):
            q = _dot(xn, wq_ref[:, 2 * h * LANES:2 * (h + 1) * LANES]).astype(BF16)
            for c, s_ref in ((0, s1_ref), (1, s2_ref)):
                to_slabs(s_ref, slice(h * PEER_KEYS, (h + 1) * PEER_KEYS),
                         _dot_nt(keys_ref[2 * h + c], q[:, c * LANES:(c + 1) * LANES]))

        def tile(j, carry):
            _peer_topk_tile(s1_ref, s2_ref, s2k_ref, bb_ref, tauw_ref, aw_ref, j)
            return carry
        lax.fori_loop(0, tt // LANES, tile, 0)
        acc_ref[...] = jnp.zeros_like(acc_ref)
        activations(hta_ref)

    mid = jnp.logical_and(e >= 1, e <= N_EC - 1)

    @pl.when(jnp.logical_and(mid, e % 2 == 1))
    def _():
        activations(htb_ref)
        gate(hta_ref, e - 1)
        project()

    @pl.when(jnp.logical_and(mid, e % 2 == 0))
    def _():
        activations(hta_ref)
        gate(htb_ref, e - 1)
        project()

    @pl.when(e == N_EC)
    def _():
        gate(htb_ref if (N_EC - 1) % 2 else hta_ref, N_EC - 1)
        project()

    @pl.when(e == N_EC)
    def _():
        peer_t = jnp.concatenate([acc_ref[j] for j in range(tt // LANES)], axis=1)
        h2 = h1_ref[...] + peer_t.T
        g = _sigmoid(_dot(_rms(h2, npg_ref[...]).astype(BF16), wpg_ref[...]))
        pe = _dot(p_ref[...].astype(BF16), wpp_ref[...])
        h3 = h2 + _rms(pe * g, npo_ref[...])
        y_ref[...] = _rms(h3, nfin_ref[...])


def _peer_ple(h1, p, w):
    t = h1.shape[0]
    tt = PEER_TT
    assert t % tt == 0
    grid = (t // tt, N_EC + 1)
    rows = PEER_HEADS * PEER_KEYS
    nl = tt // LANES
    assert rows == PEER_EC
    return pl.pallas_call(
        _peer_kernel,
        grid=grid,
        in_specs=[
            pl.BlockSpec((tt, D_MODEL), lambda i, e: (i, 0), pipeline_mode=pl.Buffered(1)),
            pl.BlockSpec((tt, PLE_DIM), lambda i, e: (i, 0), pipeline_mode=pl.Buffered(1)),
            _const_spec((1, D_MODEL)),
            _const_spec((D_MODEL, 2 * rows)),
            _const_spec((2 * PEER_HEADS, PEER_KEYS, LANES)),
            pl.BlockSpec((PEER_EC, D_MODEL), lambda i, e: (jnp.minimum(e, N_EC - 1), 0)),
            pl.BlockSpec((D_MODEL, PEER_EC), lambda i, e: (0, jnp.maximum(e - 1, 0))),
            _const_spec((1, D_MODEL)),
            _const_spec((D_MODEL, D_MODEL)),
            _const_spec((PLE_DIM, D_MODEL)),
            _const_spec((1, D_MODEL)),
            _const_spec((1, D_MODEL)),
        ],
        out_specs=pl.BlockSpec((tt, D_MODEL), lambda i, e: (i, 0)),
        out_shape=jax.ShapeDtypeStruct((t, D_MODEL), F32),
        scratch_shapes=[
            pltpu.VMEM((tt, D_MODEL), BF16),
            pltpu.VMEM((nl, rows, LANES), F32),
            pltpu.VMEM((nl, rows, LANES), F32),
            pltpu.VMEM((nl, rows, LANES), F32),
            pltpu.VMEM((nl, rows, LANES), F32),
            pltpu.VMEM((nl, PEER_EC, LANES), F32),
            pltpu.VMEM((nl, PEER_EC, LANES), F32),
            pltpu.VMEM((nl, PEER_EC, LANES), BF16),
            pltpu.VMEM((nl, D_MODEL, LANES), F32),
        ],
        compiler_params=pltpu.CompilerParams(
            dimension_semantics=("arbitrary", "arbitrary"),
            vmem_limit_bytes=VMEM_LIMIT),
        name="peer_ple",
    )(h1, p, w["norm_ffn"], w["wq"], w["keys"], w["u"], w["vt"],
      w["norm_ple_gate"], w["w_ple_gate"], w["w_ple_proj"], w["norm_ple_out"], w["norm_final"])


def _row(v):
    return v.reshape(1, -1).astype(F32)


def _prep_peer_weights(a):
    return {
        "norm_ffn": _row(a["norm_ffn"][0]),
        "wq": a["peer_wq"][0].astype(BF16),
        "keys": a["peer_keys"][0].reshape(2 * PEER_HEADS, PEER_KEYS, LANES).astype(BF16),
        "u": a["peer_u"][0].astype(BF16),
        "vt": a["peer_v"][0].astype(BF16).T,
        "norm_ple_gate": _row(a["norm_ple_gate"][0]),
        "w_ple_gate": a["w_ple_gate"][0].astype(BF16),
        "w_ple_proj": a["w_ple_proj"][0].astype(BF16),
        "norm_ple_out": _row(a["norm_ple_out"][0]),
        "norm_final": _row(a["norm_final"]),
    }


CHUNK = 128
MIX_TM = 256
HIST = SUBLANES


def _conv_from_ext(ext_ref, w, b, n):
    y = b + ext_ref[HIST - 3:HIST - 3 + n, :] * w[0:1, :]
    y = y + ext_ref[HIST - 2:HIST - 2 + n, :] * w[1:2, :]
    y = y + ext_ref[HIST - 1:HIST - 1 + n, :] * w[2:3, :]
    return y + ext_ref[HIST:HIST + n, :] * w[3:4, :]


def _lru_coeffs(xl, gates):
    wr_ref, br_ref, wi_ref, bi_ref, lam_ref = gates
    xlb = xl.astype(BF16)

    def gate_dot(w_ref):
        return jnp.concatenate(
            [_dot(xlb[:, k * MXU_DIM:(k + 1) * MXU_DIM], w_ref[k]) for k in range(w_ref.shape[0])], axis=1)

    r = _sigmoid(gate_dot(wr_ref) + br_ref[...])
    ig = _sigmoid(gate_dot(wi_ref) + bi_ref[...])
    log_a = -LRU_C * r * _softplus(-lam_ref[...])
    a_t = jnp.exp(log_a)
    mult = jnp.sqrt(-_expm1(2.0 * log_a))
    return a_t, mult * (ig * xl)


def _ssd_chunk(xbc, dt, a, st_ref, consts, dsk):
    tri_ref, exp16_ref, expall_ref, bdmask_ref = consts
    tri = tri_ref[...]
    da = dt * a
    cs = _dot3_left(tri, da)
    cs_e = _dot3(cs, exp16_ref[...])
    dt_e = _dot3(dt, exp16_ref[...])
    xs = xbc[:, 0:SSD_WIDTH]
    xdt = xs * dt_e
    bmat = xbc[:, SSD_WIDTH:SSD_WIDTH + LANES]
    cmat = xbc[:, SSD_WIDTH + LANES:SSD_WIDTH + 2 * LANES]
    last_e = cs_e[CHUNK - 1:CHUNK, :]
    xd = (xdt * jnp.exp(last_e - cs_e)).astype(BF16)
    st = st_ref[...]
    y_off = _dot(cmat.astype(BF16), st.astype(BF16)) * jnp.exp(cs_e)
    st_ref[...] = st * jnp.exp(last_e) + _dot(bmat.T.astype(BF16), xd) * bdmask_ref[...]

    col_b = _dot3(cs, expall_ref[...])
    cs_t = cs.T
    lane = lax.broadcasted_iota(jnp.int32, (CHUNK, LANES), 1)
    row = lax.broadcasted_iota(jnp.int32, (CHUNK, LANES), 0)
    causal = row >= lane
    low = lane < SSD_STATE
    bmat_b = bmat.astype(BF16)
    cb = [_dot_nt(jnp.where(low, cmat, 0.0).astype(BF16), bmat_b),
          _dot_nt(jnp.where(low, 0.0, cmat).astype(BF16), bmat_b)]
    xdt_b = xdt.astype(BF16)
    pieces = []
    for hp in range(SSD_HEADS // 2):
        cols = slice(hp * LANES, (hp + 1) * LANES)
        outs = []
        for h in (2 * hp, 2 * hp + 1):
            seg = jnp.exp(col_b[:, h * LANES:(h + 1) * LANES] - cs_t[h:h + 1, :])
            m = (cb[h // (SSD_HEADS // SSD_GROUPS)] * jnp.where(causal, seg, 0.0)).astype(BF16)
            outs.append(_dot(m, xdt_b[:, cols]))
        pieces.append(jnp.where(low, outs[0], outs[1]))
    y_diag = jnp.concatenate(pieces, axis=1)
    return y_diag + y_off + dsk * xs


def _mixer_out(x, y, z, y_lru, nssd_ref, wout_ref):
    y_ssd = _rms(y * _silu(z), nssd_ref[...])
    return (x + _dot(y_ssd.astype(BF16), wout_ref[0:SSD_WIDTH, :])
            + _dot(y_lru.astype(BF16), wout_ref[SSD_WIDTH:SSD_WIDTH + LRU_WIDTH, :]))


def _mix_p_kernel(x_ref, nmix_ref, wall_ref, cws_ref, cbs_ref, dtb_ref, alog_ref, dsk_ref, nssd_ref,
                  cwl_ref, cbl_ref, wr_ref, br_ref, wi_ref, bi_ref, lam_ref, nlru_ref, wout_ref,
                  tri_ref, exp16_ref, expall_ref, bdmask_ref,
                  h1_ref, st_out_ref, sconv_out_ref, lstate_out_ref, lconv_out_ref,
                  exts_ref, extl_ref, st_ref, hcar_ref, y_ref, pa_ref, hb_ref):
    l = pl.program_id(1)
    tm = x_ref.shape[0]

    @pl.when(l == 0)
    def _():
        exts_ref[0:HIST, :] = jnp.zeros((HIST, SSD_CONV_DIM), F32)
        extl_ref[0:HIST, :] = jnp.zeros((HIST, LRU_WIDTH), F32)
        st_ref[...] = jnp.zeros_like(st_ref)
        hcar_ref[...] = jnp.zeros_like(hcar_ref)

    x = x_ref[...]
    xn = _rms(x, nmix_ref[...]).astype(BF16)
    proj = _dot(xn, wall_ref[...])
    z = proj[:, P_Z:P_XBC]
    gate = proj[:, P_GATE:P_XL]

    exts_ref[HIST:HIST + tm, :] = proj[:, P_XBC:P_DT]
    xbc = _silu(_conv_from_ext(exts_ref, cws_ref[...], cbs_ref[...], tm))
    exts_ref[0:HIST, :] = exts_ref[tm:tm + HIST, :]
    dt = _softplus(proj[:, P_DT:P_GATE] + dtb_ref[...])
    a = -jnp.exp(alog_ref[...])
    consts = (tri_ref, exp16_ref, expall_ref, bdmask_ref)
    for c in range(tm // CHUNK):
        rs = slice(c * CHUNK, (c + 1) * CHUNK)
        y_ref[rs, :] = _ssd_chunk(xbc[rs, :], dt[rs, :], a, st_ref, consts, dsk_ref[...])

    extl_ref[HIST:HIST + tm, :] = proj[:, P_XL:P_ALL]
    xl = _conv_from_ext(extl_ref, cwl_ref[...], cbl_ref[...], tm)
    extl_ref[0:HIST, :] = extl_ref[tm:tm + HIST, :]
    a_t, bx = _lru_coeffs(xl, (wr_ref, br_ref, wi_ref, bi_ref, lam_ref))
    seg = tm // SUBLANES
    pitch = seg + SUBLANES
    hs_cols = []
    for j in range(LRU_WIDTH // LANES):
        cols = slice(j * LANES, (j + 1) * LANES)
        for s in range(SUBLANES):
            pa_ref[j, s * pitch:s * pitch + seg, :] = a_t[s * seg:(s + 1) * seg, cols]
            hb_ref[j, s * pitch:s * pitch + seg, :] = bx[s * seg:(s + 1) * seg, cols]
        hloc = jnp.zeros((SUBLANES, LANES), F32)
        ploc = jnp.ones((SUBLANES, LANES), F32)
        for tau in range(seg):
            rows = pl.ds(tau, SUBLANES, stride=pitch)
            a_r = pa_ref[j, rows, :]
            hloc = a_r * hloc + hb_ref[j, rows, :]
            ploc = a_r * ploc
            hb_ref[j, rows, :] = hloc
            pa_ref[j, rows, :] = ploc
        carry = hcar_ref[0:1, cols]
        parts = []
        for s in range(SUBLANES):
            rs = slice(s * pitch, s * pitch + seg)
            parts.append(hb_ref[j, rs, :] + pa_ref[j, rs, :] * carry)
            carry = hloc[s:s + 1, :] + ploc[s:s + 1, :] * carry
        hcar_ref[0:1, cols] = carry
        hs_cols.append(jnp.concatenate(parts, axis=0))
    hs = jnp.concatenate(hs_cols, axis=1)
    y_lru = _rms(hs * _gelu(gate), nlru_ref[...])

    h1_ref[...] = _mixer_out(x, y_ref[...], z, y_lru, nssd_ref, wout_ref)

    @pl.when(l == pl.num_programs(1) - 1)
    def _():
        st_out_ref[0] = st_ref[...]
        sconv_out_ref[0] = exts_ref[0:HIST, :]
        lstate_out_ref[0] = hcar_ref[...]
        lconv_out_ref[0] = extl_ref[0:HIST, :]


def _mix_prompt(x, w):
    b, l, _ = x.shape
    tm = MIX_TM
    assert l % tm == 0
    nl = l // tm
    names = ["norm_mix", "w_all", "conv_ssd_w", "conv_ssd_b", "dt_bias", "a_log", "d_skip", "norm_ssd",
             "conv_lru_w", "conv_lru_b", "w_r", "b_r", "w_i", "b_i", "lam", "norm_lru", "w_out",
             "tri", "exp16", "expall", "bdmask"]
    ws = [w[n] for n in names]
    gs = SSD_GROUPS * SSD_STATE
    out_shape = (
        jax.ShapeDtypeStruct((b * l, D_MODEL), F32),
        jax.ShapeDtypeStruct((b, gs, SSD_WIDTH), F32),
        jax.ShapeDtypeStruct((b, HIST, SSD_CONV_DIM), F32),
        jax.ShapeDtypeStruct((b, SUBLANES, LRU_WIDTH), F32),
        jax.ShapeDtypeStruct((b, HIST, LRU_WIDTH), F32),
    )
    per_b = lambda shape: pl.BlockSpec((1,) + shape, lambda i, j: (i, 0, 0))
    return pl.pallas_call(
        _mix_p_kernel,
        grid=(b, nl),
        in_specs=[pl.BlockSpec((tm, D_MODEL), lambda i, j: (i * nl + j, 0))]
        + [_const_spec(v.shape) for v in ws],
        out_specs=(
            pl.BlockSpec((tm, D_MODEL), lambda i, j: (i * nl + j, 0)),
            per_b((gs, SSD_WIDTH)), per_b((HIST, SSD_CONV_DIM)),
            per_b((SUBLANES, LRU_WIDTH)), per_b((HIST, LRU_WIDTH)),
        ),
        out_shape=out_shape,
        scratch_shapes=[
            pltpu.VMEM((HIST + tm, SSD_CONV_DIM), F32),
            pltpu.VMEM((HIST + tm, LRU_WIDTH), F32),
            pltpu.VMEM((gs, SSD_WIDTH), F32),
            pltpu.VMEM((SUBLANES, LRU_WIDTH), F32),
            pltpu.VMEM((tm, SSD_WIDTH), F32),
            pltpu.VMEM((LRU_WIDTH // LANES, tm + SUBLANES * SUBLANES, LANES), F32),
            pltpu.VMEM((LRU_WIDTH // LANES, tm + SUBLANES * SUBLANES, LANES), F32),
        ],
        compiler_params=pltpu.CompilerParams(
            dimension_semantics=("arbitrary", "arbitrary"),
            vmem_limit_bytes=VMEM_LIMIT),
        name="mix_prompt",
    )(x.reshape(b * l, D_MODEL), *ws)


NS = 4
NB = 128


def _mix_s1_kernel(x_ref, bufs_ref, bufl_ref, lst_ref, nmix_ref, wall_ref, cws_ref, cbs_ref, dtb_ref,
                   cwl_ref, cbl_ref, wr_ref, br_ref, wi_ref, bi_ref, lam_ref, nlru_ref,
                   z_ref, xst_ref, bct_ref, dtt_ref, ylru_ref, sconv_ref, lstate_ref, lconv_ref):
    xn = _rms(x_ref[...], nmix_ref[...]).astype(BF16)
    proj = _dot(xn, wall_ref[...])
    z_ref[...] = proj[:, P_Z:P_XBC]

    def blocks(buf_ref, raw):
        return ([buf_ref[j * NB:(j + 1) * NB, :] for j in range(CONV_W - 1)]
                + [raw[t * NB:(t + 1) * NB, :] for t in range(NS)])

    def conv(xp, w, b, t):
        y = b + xp[t] * w[0:1, :]
        for k in range(1, CONV_W):
            y = y + xp[t + k] * w[k:k + 1, :]
        return y

    xp = blocks(bufs_ref, proj[:, P_XBC:P_DT])
    for j in range(CONV_W - 1):
        sconv_ref[j * NB:(j + 1) * NB, :] = xp[NS + j]
    for t in range(NS):
        xbc = _silu(conv(xp, cws_ref[...], cbs_ref[...], t))
        dt = _softplus(proj[t * NB:(t + 1) * NB, P_DT:P_GATE] + dtb_ref[...])
        xst_ref[t] = xbc[:, 0:SSD_WIDTH].T
        bct_ref[t] = xbc[:, SSD_WIDTH:SSD_CONV_DIM].T
        dtt_ref[t] = dt.T

    xp = blocks(bufl_ref, proj[:, P_XL:P_ALL])
    for j in range(CONV_W - 1):
        lconv_ref[j * NB:(j + 1) * NB, :] = xp[NS + j]
    h = lst_ref[...]
    for t in range(NS):
        rs = slice(t * NB, (t + 1) * NB)
        xl = conv(xp, cwl_ref[...], cbl_ref[...], t)
        a_t, bx = _lru_coeffs(xl, (wr_ref, br_ref, wi_ref, bi_ref, lam_ref))
        h = a_t * h + bx
        ylru_ref[rs, :] = _rms(h * _gelu(proj[rs, P_GATE:P_XL]), nlru_ref[...])
    lstate_ref[...] = h


def _mix_s1(x_tm, bufs_tm, bufl_tm, lstate, w):
    n = NS * NB
    names = ["norm_mix", "w_all", "conv_ssd_w", "conv_ssd_b", "dt_bias",
             "conv_lru_w", "conv_lru_b", "w_r", "b_r", "w_i", "b_i", "lam", "norm_lru"]
    out_shape = (
        jax.ShapeDtypeStruct((n, SSD_WIDTH), F32),
        jax.ShapeDtypeStruct((NS, SSD_WIDTH, NB), F32),
        jax.ShapeDtypeStruct((NS, SSD_CONV_DIM - SSD_WIDTH, NB), F32),
        jax.ShapeDtypeStruct((NS, LANES, NB), F32),
        jax.ShapeDtypeStruct((n, LRU_WIDTH), F32),
        jax.ShapeDtypeStruct(((CONV_W - 1) * NB, SSD_CONV_DIM), F32),
        jax.ShapeDtypeStruct((NB, LRU_WIDTH), F32),
        jax.ShapeDtypeStruct(((CONV_W - 1) * NB, LRU_WIDTH), F32),
    )
    return pl.pallas_call(
        _mix_s1_kernel, out_shape=out_shape,
        compiler_params=pltpu.CompilerParams(vmem_limit_bytes=VMEM_LIMIT),
        name="mix_sample_in",
    )(x_tm, bufs_tm, bufl_tm, lstate, *[w[k] for k in names])


def _ssd_s_kernel(xs_ref, bc_ref, dt_ref, alog_ref, dsk_ref, s_ref, y_ref, ns_ref):
    h = pl.program_id(0)
    g = h // (SSD_HEADS // SSD_GROUPS)
    b_rows = pl.ds(pl.multiple_of(g * SSD_STATE, SSD_STATE), SSD_STATE)
    c_rows = pl.ds(pl.multiple_of(LANES + g * SSD_STATE, SSD_STATE), SSD_STATE)
    a = -jnp.exp(alog_ref[0])
    ns_ref[...] = s_ref[...]
    for t in range(NS):
        dtr = dt_ref[t, 0]
        decay = jnp.exp(dtr * a)
        bt = bc_ref[t, b_rows, :]
        ct = bc_ref[t, c_rows, :]
        xs = xs_ref[t]
        xdt = xs * dtr
        for p in range(SSD_HEAD_DIM):
            s_p = decay * ns_ref[0, p] + xdt[p:p + 1, :] * bt
            ns_ref[0, p] = s_p
            y_ref[t, p:p + 1, :] = jnp.sum(ct * s_p, axis=0, keepdims=True)
        y_ref[t] = y_ref[t] + dsk_ref[0] * xs


def _ssd_sample(xst, bct, dtt, alog_b, dsk_b, state_bm):
    return pl.pallas_call(
        _ssd_s_kernel,
        grid=(SSD_HEADS,),
        in_specs=[
            pl.BlockSpec((NS, SSD_HEAD_DIM, NB), lambda h: (0, h, 0)),
            pl.BlockSpec((NS, SSD_CONV_DIM - SSD_WIDTH, NB), lambda h: (0, 0, 0)),
            pl.BlockSpec((NS, 1, 1, NB), lambda h: (0, h, 0, 0)),
            pl.BlockSpec((1, 1, NB), lambda h: (h, 0, 0)),
            pl.BlockSpec((1, 1, NB), lambda h: (h, 0, 0)),
            pl.BlockSpec((1, SSD_HEAD_DIM, SSD_STATE, NB), lambda h: (h, 0, 0, 0)),
        ],
        out_specs=(
            pl.BlockSpec((NS, SSD_HEAD_DIM, NB), lambda h: (0, h, 0)),
            pl.BlockSpec((1, SSD_HEAD_DIM, SSD_STATE, NB), lambda h: (h, 0, 0, 0)),
        ),
        out_shape=(
            jax.ShapeDtypeStruct((NS, SSD_WIDTH, NB), F32),
            jax.ShapeDtypeStruct((SSD_HEADS, SSD_HEAD_DIM, SSD_STATE, NB), F32),
        ),
        compiler_params=pltpu.CompilerParams(dimension_semantics=("arbitrary",),
                                             vmem_limit_bytes=VMEM_LIMIT),
        name="ssd_sample",
    )(xst, bct, dtt, alog_b, dsk_b, state_bm)


def _mix_s3_kernel(x_ref, yt_ref, z_ref, ylru_ref, nssd_ref, wout_ref, h1_ref):
    for t in range(NS):
        rs = slice(t * NB, (t + 1) * NB)
        h1_ref[rs, :] = _mixer_out(x_ref[rs, :], yt_ref[t].T, z_ref[rs, :], ylru_ref[rs, :],
                                   nssd_ref, wout_ref)


def _mix_s3(x_tm, yt, z, ylru, w):
    return pl.pallas_call(
        _mix_s3_kernel,
        out_shape=jax.ShapeDtypeStruct((NS * NB, D_MODEL), F32),
        compiler_params=pltpu.CompilerParams(vmem_limit_bytes=VMEM_LIMIT),
        name="mix_sample_out",
    )(x_tm, yt, z, ylru, w["norm_ssd"], w["w_out"])


def _pad_lanes(v):
    return jnp.pad(v.reshape(1, -1).astype(F32), ((0, 0), (0, LANES - v.size)))


def _gate_tiles(wb):
    k, n, _ = wb.shape
    per = MXU_DIM // n
    return jnp.stack([_block_diag(wb[c * per:(c + 1) * per]) for c in range(k // per)]).astype(BF16)


def _block_diag(wb):
    k, n, _ = wb.shape
    eye = jnp.eye(k, dtype=wb.dtype)
    return (eye[:, None, :, None] * wb[:, :, None, :]).reshape(k * n, k * n)


def _prep_mixer_weights(a):
    w_in = a["w_in"][0]
    s1 = SSD_WIDTH
    s2 = s1 + SSD_CONV_DIM
    s3 = s2 + SSD_HEADS
    s4 = s3 + LRU_WIDTH
    pad = jnp.zeros((D_MODEL, LANES - SSD_HEADS), w_in.dtype)
    w_all = jnp.concatenate([w_in[:, :s2], w_in[:, s2:s3], pad, w_in[:, s3:]], axis=1).astype(BF16)
    hd = jnp.arange(SSD_WIDTH) // SSD_HEAD_DIM
    heads = jnp.arange(LANES)
    exp16 = (heads[:, None] == hd[None, :]).astype(BF16)
    expall = (heads[:, None] == (jnp.arange(SSD_HEADS * LANES) // LANES)[None, :]).astype(BF16)
    grp_rows = jnp.arange(SSD_GROUPS * SSD_STATE) // SSD_STATE
    grp_cols = hd // (SSD_HEADS // SSD_GROUPS)
    bdmask = (grp_rows[:, None] == grp_cols[None, :]).astype(F32)
    tri = (jnp.arange(CHUNK)[:, None] >= jnp.arange(CHUNK)[None, :]).astype(BF16)
    return {
        "norm_mix": _row(a["norm_mix"][0]),
        "w_all": w_all,
        "conv_ssd_w": a["conv_ssd_w"][0].astype(F32),
        "conv_ssd_b": _row(a["conv_ssd_b"][0]),
        "dt_bias": _pad_lanes(a["dt_bias"][0]),
        "a_log": _pad_lanes(a["a_log"][0]),
        "d_skip": _row(jnp.repeat(a["d_skip"][0], SSD_HEAD_DIM)),
        "norm_ssd": _row(a["norm_ssd"][0]),
        "conv_lru_w": a["conv_lru_w"][0].astype(F32),
        "conv_lru_b": _row(a["conv_lru_b"][0]),
        "w_r": _gate_tiles(a["w_rgate"][0]),
        "b_r": _row(a["b_rgate"][0]),
        "w_i": _gate_tiles(a["w_igate"][0]),
        "b_i": _row(a["b_igate"][0]),
        "lam": _row(a["lru_lambda"][0]),
        "norm_lru": _row(a["norm_lru"][0]),
        "w_out": a["w_out"][0].astype(BF16),
        "tri": tri, "exp16": exp16, "expall": expall, "bdmask": bdmask,
        "a_log_b": jnp.broadcast_to(a["a_log"][0].astype(F32)[:, None, None], (SSD_HEADS, 1, NB)),
        "d_skip_b": jnp.broadcast_to(a["d_skip"][0].astype(F32)[:, None, None], (SSD_HEADS, 1, NB)),
    }


def _time_major(v):
    return jnp.swapaxes(v, 0, 1).reshape(-1, v.shape[-1])


def _seq_major(v, n):
    return jnp.swapaxes(v.reshape(n, NB, v.shape[-1]), 0, 1)


def _layer_prompt(x, p, wm, wp):
    b = x.shape[0]
    h1, st, sconv, lstate, lconv = _mix_prompt(x, wm)
    y = _peer_ple(h1, p.reshape(-1, PLE_DIM), wp).reshape(x.shape)
    st5 = st.reshape(b, SSD_GROUPS, SSD_STATE, SSD_HEADS, SSD_HEAD_DIM)
    hpg = SSD_HEADS // SSD_GROUPS
    ssd_state = jnp.stack([st5[:, h // hpg, :, h, :] for h in range(SSD_HEADS)], axis=1)
    ssd_state = jnp.swapaxes(ssd_state, 2, 3)
    keep = slice(HIST - (CONV_W - 1), HIST)
    return y, ssd_state, sconv[:, keep, :], lstate[:, 0, :], lconv[:, keep, :]


def _layer_sample(x, p, s_ssd, s_sconv, s_lru, s_lconv, wm, wp):
    x_tm = _time_major(x)
    z, xst, bct, dtt, ylru, sconv, lstate, lconv = _mix_s1(
        x_tm, _time_major(s_sconv), _time_major(s_lconv), s_lru, wm)
    dtt = dtt[:, :SSD_HEADS, :].reshape(NS, SSD_HEADS, 1, NB)
    state_bm = jnp.transpose(s_ssd, (1, 2, 3, 0))
    yt, new_bm = _ssd_sample(xst, bct, dtt, wm["a_log_b"], wm["d_skip_b"], state_bm)
    h1 = _mix_s3(x_tm, yt, z, ylru, wm)
    y = _seq_major(_peer_ple(h1, _time_major(p), wp), NS)
    ssd_state = jnp.transpose(new_bm, (3, 0, 1, 2))
    return y, ssd_state, _seq_major(sconv, CONV_W - 1), lstate, _seq_major(lconv, CONV_W - 1)


def kernel(x_prompt, x_sample, state_ssd, state_ssd_conv, state_lru, state_lru_conv, p_prompt, p_sample, norm_mix, w_in, conv_ssd_w, conv_ssd_b, dt_bias, a_log, d_skip, norm_ssd, conv_lru_w, conv_lru_b, w_rgate, b_rgate, w_igate, b_igate, lru_lambda, norm_lru, w_out, norm_ffn, peer_wq, peer_keys, peer_u, peer_v, norm_ple_gate, w_ple_gate, w_ple_proj, norm_ple_out, norm_final):
    assert x_sample.shape[:2] == (NB, NS) and norm_mix.shape[0] == 1
    a = dict(norm_mix=norm_mix, w_in=w_in, conv_ssd_w=conv_ssd_w, conv_ssd_b=conv_ssd_b, dt_bias=dt_bias,
             a_log=a_log, d_skip=d_skip, norm_ssd=norm_ssd, conv_lru_w=conv_lru_w, conv_lru_b=conv_lru_b,
             w_rgate=w_rgate, b_rgate=b_rgate, w_igate=w_igate, b_igate=b_igate, lru_lambda=lru_lambda,
             norm_lru=norm_lru, w_out=w_out, norm_ffn=norm_ffn, peer_wq=peer_wq, peer_keys=peer_keys,
             peer_u=peer_u, peer_v=peer_v, norm_ple_gate=norm_ple_gate, w_ple_gate=w_ple_gate,
             w_ple_proj=w_ple_proj, norm_ple_out=norm_ple_out, norm_final=norm_final)
    wm = _prep_mixer_weights(a)
    wp = _prep_peer_weights(a)
    yp, p_ssd, p_sconv, p_lru, p_lconv = _layer_prompt(x_prompt, p_prompt[0], wm, wp)
    ys, s_ssd, s_sconv, s_lru, s_lconv = _layer_sample(
        x_sample, p_sample[0], state_ssd[0], state_ssd_conv[0], state_lru[0], state_lru_conv[0], wm, wp)
    return (yp, ys, p_ssd[None], p_sconv[None], p_lru[None], p_lconv[None],
            s_ssd[None], s_sconv[None], s_lru[None], s_lconv[None])
```

```python
import functools
import math

import jax
import jax.numpy as jnp
from jax import lax
from jax.experimental import pallas as pl
from jax.experimental.pallas import tpu as pltpu

F32 = jnp.float32
BF16 = jnp.bfloat16

D_MODEL = 1024
SSD_WIDTH = 1024
SSD_HEADS = 16
SSD_HEAD_DIM = 64
SSD_STATE = 64
SSD_GROUPS = 2
SSD_CONV_DIM = 1280
LRU_WIDTH = 1024
LRU_BLOCKS = 16
LRU_C = 8.0
CONV_W = 4
PEER_HEADS = 8
PEER_KEYS = 128
PEER_EXPERTS = PEER_KEYS * PEER_KEYS
PEER_TOPK = 16
PLE_DIM = 256
EPS = 1e-6

LANES = 128
SUBLANES = 8
MXU_DIM = 256
VMEM_LIMIT = 60 * 1024 * 1024

P_Z = 0
P_XBC = P_Z + SSD_WIDTH
P_DT = P_XBC + SSD_CONV_DIM
P_GATE = P_DT + LANES
P_XL = P_GATE + LRU_WIDTH
P_ALL = P_XL + LRU_WIDTH


def _rms(x, g):
    return x * lax.rsqrt(jnp.mean(x * x, axis=-1, keepdims=True) + EPS) * g


def _sigmoid(x):
    return 1.0 / (1.0 + jnp.exp(-x))


def _silu(x):
    return x * _sigmoid(x)


def _gelu(x):
    k1 = -2.0 * math.sqrt(2.0 / math.pi) * math.log2(math.e)
    k3 = k1 * 0.044715
    return x / (1.0 + jnp.exp2(x * (k1 + k3 * (x * x))))


def _softplus(x):
    return jnp.maximum(x, 0.0) + jnp.log1p(jnp.exp(-jnp.abs(x)))


def _expm1(x):
    u = jnp.exp(x)
    near = (u - 1.0) * x / jnp.log(u)
    return jnp.where(u == 1.0, x, jnp.where(x < -1.0, u - 1.0, near))


def _dot(a, b):
    return jnp.dot(a, b, preferred_element_type=F32)


def _dot_nt(a, b):
    return lax.dot_general(a, b, (((1,), (1,)), ((), ())), preferred_element_type=F32)


def _split3(x):
    hi = x.astype(BF16)
    r = x - hi.astype(F32)
    mid = r.astype(BF16)
    lo = (r - mid.astype(F32)).astype(BF16)
    return hi, mid, lo


def _dot3(x, m):
    hi, mid, lo = _split3(x)
    return _dot(hi, m) + _dot(mid, m) + _dot(lo, m)


def _dot3_left(m, x):
    hi, mid, lo = _split3(x)
    return _dot(m, hi) + _dot(m, mid) + _dot(m, lo)


def _const_spec(shape):
    nd = len(shape)
    return pl.BlockSpec(shape, lambda *_: (0,) * nd, pipeline_mode=pl.Buffered(1))


PEER_TT = 512
PEER_EC = 1024
N_EC = PEER_EXPERTS // PEER_EC
I_PER_EC = PEER_EC // PEER_KEYS


def _batcher_network(lo, hi):
    def merge(lo, hi, r):
        step = 2 * r
        if step < hi - lo:
            yield from merge(lo, hi, step)
            yield from merge(lo + r, hi, step)
            yield from ((i, i + r) for i in range(lo + r, hi - r, step))
        else:
            yield (lo, lo + r)
    if hi - lo >= 1:
        mid = lo + (hi - lo) // 2
        yield from _batcher_network(lo, mid)
        yield from _batcher_network(mid + 1, hi)
        yield from merge(lo, hi, 1)


SORT16 = tuple(_batcher_network(0, PEER_TOPK - 1))


def _cmpx(v, a, b):
    v[a], v[b] = jnp.maximum(v[a], v[b]), jnp.minimum(v[a], v[b])


def _top16_desc(tiles):
    v = list(tiles)
    for a, b in SORT16:
        _cmpx(v, a, b)
    for shift in (4, 2, 1):
        w = [pltpu.roll(x, shift, 0) for x in v]
        v = [jnp.maximum(v[k], w[PEER_TOPK - 1 - k]) for k in range(PEER_TOPK)]
        for d in (8, 4, 2, 1):
            for k in range(PEER_TOPK):
                if not k & d:
                    _cmpx(v, k, k + d)
    return v


def _peer_topk_tile(s1_ref, s2_ref, s2k_ref, bb_ref, tauw_ref, aw_ref, j):
    ninf = jnp.full((SUBLANES, LANES), -jnp.inf, F32)
    sub = lax.broadcasted_iota(jnp.int32, (SUBLANES, LANES), 0)

    def rows_of(vals):
        out = vals[SUBLANES - 1]
        for r in range(SUBLANES - 2, -1, -1):
            out = jnp.where(sub == r, vals[r], out)
        return out

    for h in range(PEER_HEADS):
        rs = slice(h * PEER_KEYS, (h + 1) * PEER_KEYS)
        s1 = s1_ref[j, rs, :]
        s2 = s2_ref[j, rs, :]
        t1 = _top16_desc([s1[k * SUBLANES:(k + 1) * SUBLANES, :] for k in range(PEER_TOPK)])
        t2 = _top16_desc([s2[k * SUBLANES:(k + 1) * SUBLANES, :] for k in range(PEER_TOPK)])
        t2_lo, t2_hi, t1_hi = rows_of(t2[0:8]), rows_of(t2[8:16]), rows_of(t1[8:16])
        cand = [t1[0] + t2_lo, t1[0] + t2_hi] + [t1[k1] + t2_lo for k1 in range(1, 8)] + [t1_hi + t2[0]]
        best = _top16_desc(cand + [ninf] * (PEER_TOPK - len(cand)))
        thr = best[PEER_TOPK - 1]
        z = jnp.zeros((SUBLANES, LANES), F32)
        for k in range(PEER_TOPK):
            z = z + jnp.exp(best[k] - best[0])
        inv_z = 1.0 / z
        tau = jnp.full((PEER_KEYS, LANES), jnp.inf, F32)
        for r2 in range(PEER_TOPK):
            t2_b = jnp.concatenate([t2[r2]] * PEER_TOPK, axis=0)
            tau = jnp.where(s1 + t2_b >= jnp.concatenate([thr] * PEER_TOPK, axis=0), t2_b, tau)
        a = jnp.exp(s1 - jnp.concatenate([t1[0]] * PEER_TOPK, axis=0)) * jnp.concatenate([inv_z] * PEER_TOPK, axis=0)
        s2k_ref[j, rs, :] = s2
        bb_ref[j, rs, :] = jnp.exp(s2 - jnp.concatenate([t2[0]] * PEER_TOPK, axis=0))
        for c in range(N_EC):
            src = slice(c * I_PER_EC, (c + 1) * I_PER_EC)
            dst = slice((c * PEER_HEADS + h) * I_PER_EC, (c * PEER_HEADS + h + 1) * I_PER_EC)
            tauw_ref[j, dst, :] = tau[src, :]
            aw_ref[j, dst, :] = a[src, :]


def _peer_kernel(h1_ref, p_ref, nffn_ref, wq_ref, keys_ref, u_ref, vt_ref,
                 npg_ref, wpg_ref, wpp_ref, npo_ref, nfin_ref,
                 y_ref,
                 xn_ref, s2k_ref, bb_ref, tauw_ref, aw_ref,
                 hta_ref, htb_ref, wg_ref, acc_ref):
    e = pl.program_id(1)
    tt = xn_ref.shape[0]
    half = 2 * LANES

    def to_slabs(dst_ref, rows, val):
        for j in range(val.shape[1] // LANES):
            dst_ref[j, rows, :] = val[:, j * LANES:(j + 1) * LANES]

    def activations(dst_ref, hs):
        act = _dot_nt(u_ref[...], xn_ref[hs * half:(hs + 1) * half, :])
        dst_ref[2 * hs] = act[:, 0:LANES]
        dst_ref[2 * hs + 1] = act[:, LANES:half]

    def project(wg_ref, hs):
        j0, j1 = 2 * hs, 2 * hs + 1
        out = _dot(vt_ref[...], jnp.concatenate([wg_ref[j0], wg_ref[j1]], axis=1))
        acc_ref[j0] += out[:, 0:LANES]
        acc_ref[j1] += out[:, LANES:half]

    def gate_rows(src_ref, wg_ref, c, j):
        rows_t, rows_a = [], []
        for h in range(PEER_HEADS):
            base = pl.multiple_of((c * PEER_HEADS + h) * I_PER_EC, I_PER_EC)
            rows_t.append(tauw_ref[j, pl.ds(base, I_PER_EC), :])
            rows_a.append(aw_ref[j, pl.ds(base, I_PER_EC), :])
        for il in range(I_PER_EC):
            es = slice(il * PEER_KEYS, (il + 1) * PEER_KEYS)
            w = None
            for h in range(PEER_HEADS):
                rs = slice(h * PEER_KEYS, (h + 1) * PEER_KEYS)
                tau_b = jnp.broadcast_to(rows_t[h][il:il + 1, :], (PEER_KEYS, LANES))
                a_b = jnp.broadcast_to(rows_a[h][il:il + 1, :], (PEER_KEYS, LANES))
                term = jnp.where(s2k_ref[j, rs, :] >= tau_b, bb_ref[j, rs, :], 0.0) * a_b
                w = term if w is None else w + term
            wg_ref[j, es, :] = (w * _gelu(src_ref[j, es, :])).astype(BF16)

    def stages(c, acts_to=None, gate_from=None, gate_to=None, proj_from=None):
        for hs in range(tt // half):
            if gate_from is not None:
                gate_rows(gate_from, gate_to, c, 2 * hs)
            if acts_to is not None:
                activations(acts_to, hs)
            if gate_from is not None:
                gate_rows(gate_from, gate_to, c, 2 * hs + 1)
            if proj_from is not None:
                project(proj_from, hs)

    @pl.when(e == 0)
    def _():
        xn = _rms(h1_ref[...], nffn_ref[...]).astype(BF16)
        xn_ref[...] = xn
        s1_ref, s2_ref = htb_ref, hta_ref
        for h in range(PEER_HEADS):
            q = _dot(xn, wq_ref[:, 2 * h * LANES:2 * (h + 1) * LANES]).astype(BF16)
            for c, s_ref in ((0, s1_ref), (1, s2_ref)):
                to_slabs(s_ref, slice(h * PEER_KEYS, (h + 1) * PEER_KEYS),
                         _dot_nt(keys_ref[2 * h + c], q[:, c * LANES:(c + 1) * LANES]))

        def tile(j, carry):
            _peer_topk_tile(s1_ref, s2_ref, s2k_ref, bb_ref, tauw_ref, aw_ref, j)
            return carry
        lax.fori_loop(0, tt // LANES, tile, 0)
        acc_ref[...] = jnp.zeros_like(acc_ref)
        stages(0, acts_to=hta_ref)

    ht = (hta_ref, htb_ref)
    last = N_EC - 1

    for par in (0, 1):
        @pl.when(jnp.logical_and(jnp.logical_and(e >= 1, e <= last), e % 2 == par))
        def _():
            stages(e - 1, acts_to=ht[par], gate_from=ht[1 - par], gate_to=wg_ref, proj_from=wg_ref)

    @pl.when(e == N_EC)
    def _():
        stages(last, gate_from=ht[last % 2], gate_to=wg_ref, proj_from=wg_ref)
        peer_t = jnp.concatenate([acc_ref[j] for j in range(tt // LANES)], axis=1)
        h2 = h1_ref[...] + peer_t.T
        g = _sigmoid(_dot(_rms(h2, npg_ref[...]).astype(BF16), wpg_ref[...]))
        pe = _dot(p_ref[...].astype(BF16), wpp_ref[...])
        h3 = h2 + _rms(pe * g, npo_ref[...])
        y_ref[...] = _rms(h3, nfin_ref[...])


def _peer_ple(h1, p, w):
    t = h1.shape[0]
    tt = PEER_TT
    assert t % tt == 0
    grid = (t // tt, N_EC + 1)
    rows = PEER_HEADS * PEER_KEYS
    nl = tt // LANES
    assert rows == PEER_EC
    return pl.pallas_call(
        _peer_kernel,
        grid=grid,
        in_specs=[
            pl.BlockSpec((tt, D_MODEL), lambda i, e: (i, 0), pipeline_mode=pl.Buffered(1)),
            pl.BlockSpec((tt, PLE_DIM), lambda i, e: (i, 0), pipeline_mode=pl.Buffered(1)),
            _const_spec((1, D_MODEL)),
            _const_spec((D_MODEL, 2 * rows)),
            _const_spec((2 * PEER_HEADS, PEER_KEYS, LANES)),
            pl.BlockSpec((PEER_EC, D_MODEL), lambda i, e: (jnp.minimum(e, N_EC - 1), 0)),
            pl.BlockSpec((D_MODEL, PEER_EC), lambda i, e: (0, jnp.maximum(e - 1, 0))),
            _const_spec((1, D_MODEL)),
            _const_spec((D_MODEL, D_MODEL)),
            _const_spec((PLE_DIM, D_MODEL)),
            _const_spec((1, D_MODEL)),
            _const_spec((1, D_MODEL)),
        ],
        out_specs=pl.BlockSpec((tt, D_MODEL), lambda i, e: (i, 0)),
        out_shape=jax.ShapeDtypeStruct((t, D_MODEL), F32),
        scratch_shapes=[
            pltpu.VMEM((tt, D_MODEL), BF16),
            pltpu.VMEM((nl, rows, LANES), F32),
            pltpu.VMEM((nl, rows, LANES), F32),
            pltpu.VMEM((nl, rows, LANES), F32),
            pltpu.VMEM((nl, rows, LANES), F32),
            pltpu.VMEM((nl, PEER_EC, LANES), F32),
            pltpu.VMEM((nl, PEER_EC, LANES), F32),
            pltpu.VMEM((nl, PEER_EC, LANES), BF16),
            pltpu.VMEM((nl, D_MODEL, LANES), F32),
        ],
        compiler_params=pltpu.CompilerParams(
            dimension_semantics=("arbitrary", "arbitrary"),
            vmem_limit_bytes=VMEM_LIMIT),
        name="peer_ple",
    )(h1, p, w["norm_ffn"], w["wq"], w["keys"], w["u"], w["vt"],
      w["norm_ple_gate"], w["w_ple_gate"], w["w_ple_proj"], w["norm_ple_out"], w["norm_final"])


def _row(v):
    return v.reshape(1, -1).astype(F32)


def _prep_peer_weights(a):
    return {
        "norm_ffn": _row(a["norm_ffn"][0]),
        "wq": a["peer_wq"][0].astype(BF16),
        "keys": a["peer_keys"][0].reshape(2 * PEER_HEADS, PEER_KEYS, LANES).astype(BF16),
        "u": a["peer_u"][0].astype(BF16),
        "vt": a["peer_v"][0].astype(BF16).T,
        "norm_ple_gate": _row(a["norm_ple_gate"][0]),
        "w_ple_gate": a["w_ple_gate"][0].astype(BF16),
        "w_ple_proj": a["w_ple_proj"][0].astype(BF16),
        "norm_ple_out": _row(a["norm_ple_out"][0]),
        "norm_final": _row(a["norm_final"]),
    }


CHUNK = 128
MIX_TM = 256
HIST = SUBLANES


def _conv_from_ext(ext_ref, w, b, n):
    y = b + ext_ref[HIST - 3:HIST - 3 + n, :] * w[0:1, :]
    y = y + ext_ref[HIST - 2:HIST - 2 + n, :] * w[1:2, :]
    y = y + ext_ref[HIST - 1:HIST - 1 + n, :] * w[2:3, :]
    return y + ext_ref[HIST:HIST + n, :] * w[3:4, :]


def _lru_coeffs(xl, gates):
    wr_ref, br_ref, wi_ref, bi_ref, lam_ref = gates
    xlb = xl.astype(BF16)

    def gate_dot(w_ref):
        return jnp.concatenate(
            [_dot(xlb[:, k * MXU_DIM:(k + 1) * MXU_DIM], w_ref[k]) for k in range(w_ref.shape[0])], axis=1)

    r = _sigmoid(gate_dot(wr_ref) + br_ref[...])
    ig = _sigmoid(gate_dot(wi_ref) + bi_ref[...])
    log_a = -LRU_C * r * _softplus(-lam_ref[...])
    a_t = jnp.exp(log_a)
    mult = jnp.sqrt(-_expm1(2.0 * log_a))
    return a_t, mult * (ig * xl)


def _ssd_chunk(xbc, dt, a, st_ref, consts, dsk):
    tri_ref, exp16_ref, expall_ref, bdmask_ref = consts
    tri = tri_ref[...]
    da = dt * a
    cs = _dot3_left(tri, da)
    cs_e = _dot3(cs, exp16_ref[...])
    dt_e = _dot3(dt, exp16_ref[...])
    xs = xbc[:, 0:SSD_WIDTH]
    xdt = xs * dt_e
    bmat = xbc[:, SSD_WIDTH:SSD_WIDTH + LANES]
    cmat = xbc[:, SSD_WIDTH + LANES:SSD_WIDTH + 2 * LANES]
    last_e = cs_e[CHUNK - 1:CHUNK, :]
    xd = (xdt * jnp.exp(last_e - cs_e)).astype(BF16)
    st = st_ref[...]
    y_off = _dot(cmat.astype(BF16), st.astype(BF16)) * jnp.exp(cs_e)
    st_ref[...] = st * jnp.exp(last_e) + _dot(bmat.T.astype(BF16), xd) * bdmask_ref[...]

    col_b = _dot3(cs, expall_ref[...])
    cs_t = cs.T
    lane = lax.broadcasted_iota(jnp.int32, (CHUNK, LANES), 1)
    row = lax.broadcasted_iota(jnp.int32, (CHUNK, LANES), 0)
    causal = row >= lane
    low = lane < SSD_STATE
    bmat_b = bmat.astype(BF16)
    cb = [_dot_nt(jnp.where(low, cmat, 0.0).astype(BF16), bmat_b),
          _dot_nt(jnp.where(low, 0.0, cmat).astype(BF16), bmat_b)]
    xdt_b = xdt.astype(BF16)
    pieces = []
    for hp in range(SSD_HEADS // 2):
        cols = slice(hp * LANES, (hp + 1) * LANES)
        outs = []
        for h in (2 * hp, 2 * hp + 1):
            seg = jnp.exp(col_b[:, h * LANES:(h + 1) * LANES] - cs_t[h:h + 1, :])
            m = (cb[h // (SSD_HEADS // SSD_GROUPS)] * jnp.where(causal, seg, 0.0)).astype(BF16)
            outs.append(_dot(m, xdt_b[:, cols]))
        pieces.append(jnp.where(low, outs[0], outs[1]))
    y_diag = jnp.concatenate(pieces, axis=1)
    return y_diag + y_off + dsk * xs


def _mixer_out(x, y, z, y_lru, nssd_ref, wout_ref):
    y_ssd = _rms(y * _silu(z), nssd_ref[...])
    return (x + _dot(y_ssd.astype(BF16), wout_ref[0:SSD_WIDTH, :])
            + _dot(y_lru.astype(BF16), wout_ref[SSD_WIDTH:SSD_WIDTH + LRU_WIDTH, :]))


def _mix_p_kernel(x_ref, nmix_ref, wall_ref, cws_ref, cbs_ref, dtb_ref, alog_ref, dsk_ref, nssd_ref,
                  cwl_ref, cbl_ref, wr_ref, br_ref, wi_ref, bi_ref, lam_ref, nlru_ref, wout_ref,
                  tri_ref, exp16_ref, expall_ref, bdmask_ref,
                  h1_ref, st_out_ref, sconv_out_ref, lstate_out_ref, lconv_out_ref,
                  exts_ref, extl_ref, st_ref, hcar_ref, y_ref, pa_ref, hb_ref):
    l = pl.program_id(1)
    tm = x_ref.shape[0]

    @pl.when(l == 0)
    def _():
        exts_ref[0:HIST, :] = jnp.zeros((HIST, SSD_CONV_DIM), F32)
        extl_ref[0:HIST, :] = jnp.zeros((HIST, LRU_WIDTH), F32)
        st_ref[...] = jnp.zeros_like(st_ref)
        hcar_ref[...] = jnp.zeros_like(hcar_ref)

    x = x_ref[...]
    xn = _rms(x, nmix_ref[...]).astype(BF16)

    def proj(lo, hi):
        return _dot(xn, wall_ref[:, lo:hi])

    exts_ref[HIST:HIST + tm, :] = proj(P_XBC, P_DT)
    xbc = _silu(_conv_from_ext(exts_ref, cws_ref[...], cbs_ref[...], tm))
    exts_ref[0:HIST, :] = exts_ref[tm:tm + HIST, :]
    dt = _softplus(proj(P_DT, P_GATE) + dtb_ref[...])
    a = -jnp.exp(alog_ref[...])
    consts = (tri_ref, exp16_ref, expall_ref, bdmask_ref)
    extl_ref[HIST:HIST + tm, :] = proj(P_XL, P_ALL)
    for c in range(tm // CHUNK):
        rs = slice(c * CHUNK, (c + 1) * CHUNK)
        y_ref[rs, :] = _ssd_chunk(xbc[rs, :], dt[rs, :], a, st_ref, consts, dsk_ref[...])
    z = proj(P_Z, P_XBC)
    gate = proj(P_GATE, P_XL)

    xl = _conv_from_ext(extl_ref, cwl_ref[...], cbl_ref[...], tm)
    extl_ref[0:HIST, :] = extl_ref[tm:tm + HIST, :]
    a_t, bx = _lru_coeffs(xl, (wr_ref, br_ref, wi_ref, bi_ref, lam_ref))
    seg = tm // SUBLANES
    pitch = seg + SUBLANES
    hs_cols = []
    for j in range(LRU_WIDTH // LANES):
        cols = slice(j * LANES, (j + 1) * LANES)
        for s in range(SUBLANES):
            pa_ref[j, s * pitch:s * pitch + seg, :] = a_t[s * seg:(s + 1) * seg, cols]
            hb_ref[j, s * pitch:s * pitch + seg, :] = bx[s * seg:(s + 1) * seg, cols]
        hloc = jnp.zeros((SUBLANES, LANES), F32)
        ploc = jnp.ones((SUBLANES, LANES), F32)
        for tau in range(seg):
            rows = pl.ds(tau, SUBLANES, stride=pitch)
            a_r = pa_ref[j, rows, :]
            hloc = a_r * hloc + hb_ref[j, rows, :]
            ploc = a_r * ploc
            hb_ref[j, rows, :] = hloc
            pa_ref[j, rows, :] = ploc
        carry = hcar_ref[0:1, cols]
        parts = []
        for s in range(SUBLANES):
            rs = slice(s * pitch, s * pitch + seg)
            parts.append(hb_ref[j, rs, :] + pa_ref[j, rs, :] * carry)
            carry = hloc[s:s + 1, :] + ploc[s:s + 1, :] * carry
        hcar_ref[0:1, cols] = carry
        hs_cols.append(jnp.concatenate(parts, axis=0))
    hs = jnp.concatenate(hs_cols, axis=1)
    y_lru = _rms(hs * _gelu(gate), nlru_ref[...])

    h1_ref[...] = _mixer_out(x, y_ref[...], z, y_lru, nssd_ref, wout_ref)

    @pl.when(l == pl.num_programs(1) - 1)
    def _():
        st_out_ref[0] = st_ref[...]
        sconv_out_ref[0] = exts_ref[0:HIST, :]
        lstate_out_ref[0] = hcar_ref[...]
        lconv_out_ref[0] = extl_ref[0:HIST, :]


def _mix_prompt(x, w):
    b, l, _ = x.shape
    tm = MIX_TM
    assert l % tm == 0
    nl = l // tm
    names = ["norm_mix", "w_all", "conv_ssd_w", "conv_ssd_b", "dt_bias", "a_log", "d_skip", "norm_ssd",
             "conv_lru_w", "conv_lru_b", "w_r", "b_r", "w_i", "b_i", "lam", "norm_lru", "w_out",
             "tri", "exp16", "expall", "bdmask"]
    ws = [w[n] for n in names]
    gs = SSD_GROUPS * SSD_STATE
    out_shape = (
        jax.ShapeDtypeStruct((b * l, D_MODEL), F32),
        jax.ShapeDtypeStruct((b, gs, SSD_WIDTH), F32),
        jax.ShapeDtypeStruct((b, HIST, SSD_CONV_DIM), F32),
        jax.ShapeDtypeStruct((b, SUBLANES, LRU_WIDTH), F32),
        jax.ShapeDtypeStruct((b, HIST, LRU_WIDTH), F32),
    )
    per_b = lambda shape: pl.BlockSpec((1,) + shape, lambda i, j: (i, 0, 0))
    return pl.pallas_call(
        _mix_p_kernel,
        grid=(b, nl),
        in_specs=[pl.BlockSpec((tm, D_MODEL), lambda i, j: (i * nl + j, 0))]
        + [_const_spec(v.shape) for v in ws],
        out_specs=(
            pl.BlockSpec((tm, D_MODEL), lambda i, j: (i * nl + j, 0)),
            per_b((gs, SSD_WIDTH)), per_b((HIST, SSD_CONV_DIM)),
            per_b((SUBLANES, LRU_WIDTH)), per_b((HIST, LRU_WIDTH)),
        ),
        out_shape=out_shape,
        scratch_shapes=[
            pltpu.VMEM((HIST + tm, SSD_CONV_DIM), F32),
            pltpu.VMEM((HIST + tm, LRU_WIDTH), F32),
            pltpu.VMEM((gs, SSD_WIDTH), F32),
            pltpu.VMEM((SUBLANES, LRU_WIDTH), F32),
            pltpu.VMEM((tm, SSD_WIDTH), F32),
            pltpu.VMEM((LRU_WIDTH // LANES, tm + SUBLANES * SUBLANES, LANES), F32),
            pltpu.VMEM((LRU_WIDTH // LANES, tm + SUBLANES * SUBLANES, LANES), F32),
        ],
        compiler_params=pltpu.CompilerParams(
            dimension_semantics=("arbitrary", "arbitrary"),
            vmem_limit_bytes=VMEM_LIMIT),
        name="mix_prompt",
    )(x.reshape(b * l, D_MODEL), *ws)


NS = 4
NB = 128


def _mix_s1_kernel(x_ref, bufs_ref, bufl_ref, lst_ref, nmix_ref, wall_ref, cws_ref, cbs_ref, dtb_ref,
                   cwl_ref, cbl_ref, wr_ref, br_ref, wi_ref, bi_ref, lam_ref, nlru_ref,
                   z_ref, xst_ref, bct_ref, dtt_ref, ylru_ref, sconv_ref, lstate_ref, lconv_ref):
    xn = _rms(x_ref[...], nmix_ref[...]).astype(BF16)
    proj = _dot(xn, wall_ref[...])
    z_ref[...] = proj[:, P_Z:P_XBC]

    def blocks(buf_ref, raw):
        return ([buf_ref[j * NB:(j + 1) * NB, :] for j in range(CONV_W - 1)]
                + [raw[t * NB:(t + 1) * NB, :] for t in range(NS)])

    def conv(xp, w, b, t):
        y = b + xp[t] * w[0:1, :]
        for k in range(1, CONV_W):
            y = y + xp[t + k] * w[k:k + 1, :]
        return y

    xp = blocks(bufs_ref, proj[:, P_XBC:P_DT])
    for j in range(CONV_W - 1):
        sconv_ref[j * NB:(j + 1) * NB, :] = xp[NS + j]
    for t in range(NS):
        xbc = _silu(conv(xp, cws_ref[...], cbs_ref[...], t))
        dt = _softplus(proj[t * NB:(t + 1) * NB, P_DT:P_GATE] + dtb_ref[...])
        xst_ref[t] = xbc[:, 0:SSD_WIDTH].T
        bct_ref[t] = xbc[:, SSD_WIDTH:SSD_CONV_DIM].T
        dtt_ref[t] = dt.T

    xp = blocks(bufl_ref, proj[:, P_XL:P_ALL])
    for j in range(CONV_W - 1):
        lconv_ref[j * NB:(j + 1) * NB, :] = xp[NS + j]
    h = lst_ref[...]
    for t in range(NS):
        rs = slice(t * NB, (t + 1) * NB)
        xl = conv(xp, cwl_ref[...], cbl_ref[...], t)
        a_t, bx = _lru_coeffs(xl, (wr_ref, br_ref, wi_ref, bi_ref, lam_ref))
        h = a_t * h + bx
        ylru_ref[rs, :] = _rms(h * _gelu(proj[rs, P_GATE:P_XL]), nlru_ref[...])
    lstate_ref[...] = h


def _mix_s1(x_tm, bufs_tm, bufl_tm, lstate, w):
    n = NS * NB
    names = ["norm_mix", "w_all", "conv_ssd_w", "conv_ssd_b", "dt_bias",
             "conv_lru_w", "conv_lru_b", "w_r", "b_r", "w_i", "b_i", "lam", "norm_lru"]
    out_shape = (
        jax.ShapeDtypeStruct((n, SSD_WIDTH), F32),
        jax.ShapeDtypeStruct((NS, SSD_WIDTH, NB), F32),
        jax.ShapeDtypeStruct((NS, SSD_CONV_DIM - SSD_WIDTH, NB), F32),
        jax.ShapeDtypeStruct((NS, LANES, NB), F32),
        jax.ShapeDtypeStruct((n, LRU_WIDTH), F32),
        jax.ShapeDtypeStruct(((CONV_W - 1) * NB, SSD_CONV_DIM), F32),
        jax.ShapeDtypeStruct((NB, LRU_WIDTH), F32),
        jax.ShapeDtypeStruct(((CONV_W - 1) * NB, LRU_WIDTH), F32),
    )
    return pl.pallas_call(
        _mix_s1_kernel, out_shape=out_shape,
        compiler_params=pltpu.CompilerParams(vmem_limit_bytes=VMEM_LIMIT),
        name="mix_sample_in",
    )(x_tm, bufs_tm, bufl_tm, lstate, *[w[k] for k in names])


def _ssd_s_kernel(xs_ref, bc_ref, dt_ref, alog_ref, dsk_ref, s_ref, y_ref, ns_ref):
    h = pl.program_id(0)
    g = h // (SSD_HEADS // SSD_GROUPS)
    b_rows = pl.ds(pl.multiple_of(g * SSD_STATE, SSD_STATE), SSD_STATE)
    c_rows = pl.ds(pl.multiple_of(LANES + g * SSD_STATE, SSD_STATE), SSD_STATE)
    a = -jnp.exp(alog_ref[0])
    ns_ref[...] = s_ref[...]
    for t in range(NS):
        dtr = dt_ref[t, 0]
        decay = jnp.exp(dtr * a)
        bt = bc_ref[t, b_rows, :]
        ct = bc_ref[t, c_rows, :]
        xs = xs_ref[t]
        xdt = xs * dtr
        for p in range(SSD_HEAD_DIM):
            s_p = decay * ns_ref[0, p] + xdt[p:p + 1, :] * bt
            ns_ref[0, p] = s_p
            y_ref[t, p:p + 1, :] = jnp.sum(ct * s_p, axis=0, keepdims=True)
        y_ref[t] = y_ref[t] + dsk_ref[0] * xs


def _ssd_sample(xst, bct, dtt, alog_b, dsk_b, state_bm):
    return pl.pallas_call(
        _ssd_s_kernel,
        grid=(SSD_HEADS,),
        in_specs=[
            pl.BlockSpec((NS, SSD_HEAD_DIM, NB), lambda h: (0, h, 0)),
            pl.BlockSpec((NS, SSD_CONV_DIM - SSD_WIDTH, NB), lambda h: (0, 0, 0)),
            pl.BlockSpec((NS, 1, 1, NB), lambda h: (0, h, 0, 0)),
            pl.BlockSpec((1, 1, NB), lambda h: (h, 0, 0)),
            pl.BlockSpec((1, 1, NB), lambda h: (h, 0, 0)),
            pl.BlockSpec((1, SSD_HEAD_DIM, SSD_STATE, NB), lambda h: (h, 0, 0, 0)),
        ],
        out_specs=(
            pl.BlockSpec((NS, SSD_HEAD_DIM, NB), lambda h: (0, h, 0)),
            pl.BlockSpec((1, SSD_HEAD_DIM, SSD_STATE, NB), lambda h: (h, 0, 0, 0)),
        ),
        out_shape=(
            jax.ShapeDtypeStruct((NS, SSD_WIDTH, NB), F32),
            jax.ShapeDtypeStruct((SSD_HEADS, SSD_HEAD_DIM, SSD_STATE, NB), F32),
        ),
        compiler_params=pltpu.CompilerParams(dimension_semantics=("arbitrary",),
                                             vmem_limit_bytes=VMEM_LIMIT),
        name="ssd_sample",
    )(xst, bct, dtt, alog_b, dsk_b, state_bm)


def _mix_s3_kernel(x_ref, yt_ref, z_ref, ylru_ref, nssd_ref, wout_ref, h1_ref):
    for t in range(NS):
        rs = slice(t * NB, (t + 1) * NB)
        h1_ref[rs, :] = _mixer_out(x_ref[rs, :], yt_ref[t].T, z_ref[rs, :], ylru_ref[rs, :],
                                   nssd_ref, wout_ref)


def _mix_s3(x_tm, yt, z, ylru, w):
    return pl.pallas_call(
        _mix_s3_kernel,
        out_shape=jax.ShapeDtypeStruct((NS * NB, D_MODEL), F32),
        compiler_params=pltpu.CompilerParams(vmem_limit_bytes=VMEM_LIMIT),
        name="mix_sample_out",
    )(x_tm, yt, z, ylru, w["norm_ssd"], w["w_out"])


def _pad_lanes(v):
    return jnp.pad(v.reshape(1, -1).astype(F32), ((0, 0), (0, LANES - v.size)))


def _gate_tiles(wb):
    k, n, _ = wb.shape
    per = MXU_DIM // n
    return jnp.stack([_block_diag(wb[c * per:(c + 1) * per]) for c in range(k // per)]).astype(BF16)


def _block_diag(wb):
    k, n, _ = wb.shape
    eye = jnp.eye(k, dtype=wb.dtype)
    return (eye[:, None, :, None] * wb[:, :, None, :]).reshape(k * n, k * n)


def _prep_mixer_weights(a):
    w_in = a["w_in"][0]
    s1 = SSD_WIDTH
    s2 = s1 + SSD_CONV_DIM
    s3 = s2 + SSD_HEADS
    s4 = s3 + LRU_WIDTH
    pad = jnp.zeros((D_MODEL, LANES - SSD_HEADS), w_in.dtype)
    w_all = jnp.concatenate([w_in[:, :s2], w_in[:, s2:s3], pad, w_in[:, s3:]], axis=1).astype(BF16)
    hd = jnp.arange(SSD_WIDTH) // SSD_HEAD_DIM
    heads = jnp.arange(LANES)
    exp16 = (heads[:, None] == hd[None, :]).astype(BF16)
    expall = (heads[:, None] == (jnp.arange(SSD_HEADS * LANES) // LANES)[None, :]).astype(BF16)
    grp_rows = jnp.arange(SSD_GROUPS * SSD_STATE) // SSD_STATE
    grp_cols = hd // (SSD_HEADS // SSD_GROUPS)
    bdmask = (grp_rows[:, None] == grp_cols[None, :]).astype(F32)
    tri = (jnp.arange(CHUNK)[:, None] >= jnp.arange(CHUNK)[None, :]).astype(BF16)
    return {
        "norm_mix": _row(a["norm_mix"][0]),
        "w_all": w_all,
        "conv_ssd_w": a["conv_ssd_w"][0].astype(F32),
        "conv_ssd_b": _row(a["conv_ssd_b"][0]),
        "dt_bias": _pad_lanes(a["dt_bias"][0]),
        "a_log": _pad_lanes(a["a_log"][0]),
        "d_skip": _row(jnp.repeat(a["d_skip"][0], SSD_HEAD_DIM)),
        "norm_ssd": _row(a["norm_ssd"][0]),
        "conv_lru_w": a["conv_lru_w"][0].astype(F32),
        "conv_lru_b": _row(a["conv_lru_b"][0]),
        "w_r": _gate_tiles(a["w_rgate"][0]),
        "b_r": _row(a["b_rgate"][0]),
        "w_i": _gate_tiles(a["w_igate"][0]),
        "b_i": _row(a["b_igate"][0]),
        "lam": _row(a["lru_lambda"][0]),
        "norm_lru": _row(a["norm_lru"][0]),
        "w_out": a["w_out"][0].astype(BF16),
        "tri": tri, "exp16": exp16, "expall": expall, "bdmask": bdmask,
        "a_log_b": jnp.broadcast_to(a["a_log"][0].astype(F32)[:, None, None], (SSD_HEADS, 1, NB)),
        "d_skip_b": jnp.broadcast_to(a["d_skip"][0].astype(F32)[:, None, None], (SSD_HEADS, 1, NB)),
    }


def _time_major(v):
    return jnp.swapaxes(v, 0, 1).reshape(-1, v.shape[-1])


def _seq_major(v, n):
    return jnp.swapaxes(v.reshape(n, NB, v.shape[-1]), 0, 1)


def _layer_prompt(x, p, wm, wp):
    b = x.shape[0]
    h1, st, sconv, lstate, lconv = _mix_prompt(x, wm)
    y = _peer_ple(h1, p.reshape(-1, PLE_DIM), wp).reshape(x.shape)
    st5 = st.reshape(b, SSD_GROUPS, SSD_STATE, SSD_HEADS, SSD_HEAD_DIM)
    hpg = SSD_HEADS // SSD_GROUPS
    ssd_state = jnp.stack([st5[:, h // hpg, :, h, :] for h in range(SSD_HEADS)], axis=1)
    ssd_state = jnp.swapaxes(ssd_state, 2, 3)
    keep = slice(HIST - (CONV_W - 1), HIST)
    return y, ssd_state, sconv[:, keep, :], lstate[:, 0, :], lconv[:, keep, :]


def _layer_sample(x, p, s_ssd, s_sconv, s_lru, s_lconv, wm, wp):
    x_tm = _time_major(x)
    z, xst, bct, dtt, ylru, sconv, lstate, lconv = _mix_s1(
        x_tm, _time_major(s_sconv), _time_major(s_lconv), s_lru, wm)
    dtt = dtt[:, :SSD_HEADS, :].reshape(NS, SSD_HEADS, 1, NB)
    state_bm = jnp.transpose(s_ssd, (1, 2, 3, 0))
    yt, new_bm = _ssd_sample(xst, bct, dtt, wm["a_log_b"], wm["d_skip_b"], state_bm)
    h1 = _mix_s3(x_tm, yt, z, ylru, wm)
    y = _seq_major(_peer_ple(h1, _time_major(p), wp), NS)
    ssd_state = jnp.transpose(new_bm, (3, 0, 1, 2))
    return y, ssd_state, _seq_major(sconv, CONV_W - 1), lstate, _seq_major(lconv, CONV_W - 1)


def kernel(x_prompt, x_sample, state_ssd, state_ssd_conv, state_lru, state_lru_conv, p_prompt, p_sample, norm_mix, w_in, conv_ssd_w, conv_ssd_b, dt_bias, a_log, d_skip, norm_ssd, conv_lru_w, conv_lru_b, w_rgate, b_rgate, w_igate, b_igate, lru_lambda, norm_lru, w_out, norm_ffn, peer_wq, peer_keys, peer_u, peer_v, norm_ple_gate, w_ple_gate, w_ple_proj, norm_ple_out, norm_final):
    assert x_sample.shape[:2] == (NB, NS) and norm_mix.shape[0] == 1
    a = dict(norm_mix=norm_mix, w_in=w_in, conv_ssd_w=conv_ssd_w, conv_ssd_b=conv_ssd_b, dt_bias=dt_bias,
             a_log=a_log, d_skip=d_skip, norm_ssd=norm_ssd, conv_lru_w=conv_lru_w, conv_lru_b=conv_lru_b,
             w_rgate=w_rgate, b_rgate=b_rgate, w_igate=w_igate, b_igate=b_igate, lru_lambda=lru_lambda,
             norm_lru=norm_lru, w_out=w_out, norm_ffn=norm_ffn, peer_wq=peer_wq, peer_keys=peer_keys,
             peer_u=peer_u, peer_v=peer_v, norm_ple_gate=norm_ple_gate, w_ple_gate=w_ple_gate,
             w_ple_proj=w_ple_proj, norm_ple_out=norm_ple_out, norm_final=norm_final)
    wm = _prep_mixer_weights(a)
    wp = _prep_peer_weights(a)
    yp, p_ssd, p_sconv, p_lru, p_lconv = _layer_prompt(x_prompt, p_prompt[0], wm, wp)
    ys, s_ssd, s_sconv, s_lru, s_lconv = _layer_sample(
        x_sample, p_sample[0], state_ssd[0], state_ssd_conv[0], state_lru[0], state_lru_conv[0], wm, wp)
    return (yp, ys, p_ssd[None], p_sconv[None], p_lru[None], p_lconv[None],
            s_ssd[None], s_sconv[None], s_lru[None], s_lconv[None])
```

```python
import functools
import math

import jax
import jax.numpy as jnp
from jax import lax
from jax.experimental import pallas as pl
from jax.experimental.pallas import tpu as pltpu

F32 = jnp.float32
BF16 = jnp.bfloat16

D_MODEL = 1024
SSD_WIDTH = 1024
SSD_HEADS = 16
SSD_HEAD_DIM = 64
SSD_STATE = 64
SSD_GROUPS = 2
SSD_CONV_DIM = 1280
LRU_WIDTH = 1024
LRU_BLOCKS = 16
LRU_C = 8.0
CONV_W = 4
PEER_HEADS = 8
PEER_KEYS = 128
PEER_EXPERTS = PEER_KEYS * PEER_KEYS
PEER_TOPK = 16
PLE_DIM = 256
EPS = 1e-6

LANES = 128
SUBLANES = 8
MXU_DIM = 256
VMEM_LIMIT = 60 * 1024 * 1024

P_Z = 0
P_XBC = P_Z + SSD_WIDTH
P_DT = P_XBC + SSD_CONV_DIM
P_GATE = P_DT + LANES
P_XL = P_GATE + LRU_WIDTH
P_ALL = P_XL + LRU_WIDTH


def _rms(x, g):
    return x * lax.rsqrt(jnp.mean(x * x, axis=-1, keepdims=True) + EPS) * g


def _sigmoid(x):
    return 1.0 / (1.0 + jnp.exp(-x))


def _silu(x):
    return x * _sigmoid(x)


def _gelu(x):
    k1 = -2.0 * math.sqrt(2.0 / math.pi) * math.log2(math.e)
    k3 = k1 * 0.044715
    return x / (1.0 + jnp.exp2(x * (k1 + k3 * (x * x))))


def _softplus(x):
    return jnp.maximum(x, 0.0) + jnp.log1p(jnp.exp(-jnp.abs(x)))


def _expm1(x):
    u = jnp.exp(x)
    near = (u - 1.0) * x / jnp.log(u)
    return jnp.where(u == 1.0, x, jnp.where(x < -1.0, u - 1.0, near))


def _dot(a, b):
    return jnp.dot(a, b, preferred_element_type=F32)


def _dot_nt(a, b):
    return lax.dot_general(a, b, (((1,), (1,)), ((), ())), preferred_element_type=F32)


def _split3(x):
    hi = x.astype(BF16)
    r = x - hi.astype(F32)
    mid = r.astype(BF16)
    lo = (r - mid.astype(F32)).astype(BF16)
    return hi, mid, lo


def _dot3_left(m, x):
    hi, mid, lo = _split3(x)
    return _dot(m, hi) + _dot(m, mid) + _dot(m, lo)


def _const_spec(shape):
    nd = len(shape)
    return pl.BlockSpec(shape, lambda *_: (0,) * nd, pipeline_mode=pl.Buffered(1))


PEER_TT = 512
PEER_EC = 1024
N_EC = PEER_EXPERTS // PEER_EC
I_PER_EC = PEER_EC // PEER_KEYS


def _batcher_network(lo, hi):
    def merge(lo, hi, r):
        step = 2 * r
        if step < hi - lo:
            yield from merge(lo, hi, step)
            yield from merge(lo + r, hi, step)
            yield from ((i, i + r) for i in range(lo + r, hi - r, step))
        else:
            yield (lo, lo + r)
    if hi - lo >= 1:
        mid = lo + (hi - lo) // 2
        yield from _batcher_network(lo, mid)
        yield from _batcher_network(mid + 1, hi)
        yield from merge(lo, hi, 1)


SORT16 = tuple(_batcher_network(0, PEER_TOPK - 1))


def _cmpx(v, a, b):
    v[a], v[b] = jnp.maximum(v[a], v[b]), jnp.minimum(v[a], v[b])


def _top16_desc(tiles):
    v = list(tiles)
    for a, b in SORT16:
        _cmpx(v, a, b)
    for shift in (4, 2, 1):
        w = [pltpu.roll(x, shift, 0) for x in v]
        v = [jnp.maximum(v[k], w[PEER_TOPK - 1 - k]) for k in range(PEER_TOPK)]
        for d in (8, 4, 2, 1):
            for k in range(PEER_TOPK):
                if not k & d:
                    _cmpx(v, k, k + d)
    return v


def _peer_topk_tile(s1_ref, s2_ref, s2k_ref, bb_ref, tauw_ref, aw_ref, j):
    ninf = jnp.full((SUBLANES, LANES), -jnp.inf, F32)
    sub = lax.broadcasted_iota(jnp.int32, (SUBLANES, LANES), 0)

    def rows_of(vals):
        out = vals[SUBLANES - 1]
        for r in range(SUBLANES - 2, -1, -1):
            out = jnp.where(sub == r, vals[r], out)
        return out

    for h in range(PEER_HEADS):
        rs = slice(h * PEER_KEYS, (h + 1) * PEER_KEYS)
        s1 = s1_ref[j, rs, :]
        s2 = s2_ref[j, rs, :]
        t1 = _top16_desc([s1[k * SUBLANES:(k + 1) * SUBLANES, :] for k in range(PEER_TOPK)])
        t2 = _top16_desc([s2[k * SUBLANES:(k + 1) * SUBLANES, :] for k in range(PEER_TOPK)])
        t2_lo, t2_hi, t1_hi = rows_of(t2[0:8]), rows_of(t2[8:16]), rows_of(t1[8:16])
        cand = [t1[0] + t2_lo, t1[0] + t2_hi] + [t1[k1] + t2_lo for k1 in range(1, 8)] + [t1_hi + t2[0]]
        best = _top16_desc(cand + [ninf] * (PEER_TOPK - len(cand)))
        thr = best[PEER_TOPK - 1]
        z = jnp.zeros((SUBLANES, LANES), F32)
        for k in range(PEER_TOPK):
            z = z + jnp.exp(best[k] - best[0])
        inv_z = 1.0 / z
        tau = jnp.full((PEER_KEYS, LANES), jnp.inf, F32)
        for r2 in range(PEER_TOPK):
            t2_b = jnp.concatenate([t2[r2]] * PEER_TOPK, axis=0)
            tau = jnp.where(s1 + t2_b >= jnp.concatenate([thr] * PEER_TOPK, axis=0), t2_b, tau)
        a = jnp.exp(s1 - jnp.concatenate([t1[0]] * PEER_TOPK, axis=0)) * jnp.concatenate([inv_z] * PEER_TOPK, axis=0)
        s2k_ref[j, rs, :] = s2
        bb_ref[j, rs, :] = jnp.exp(s2 - jnp.concatenate([t2[0]] * PEER_TOPK, axis=0))
        for c in range(N_EC):
            src = slice(c * I_PER_EC, (c + 1) * I_PER_EC)
            dst = slice((c * PEER_HEADS + h) * I_PER_EC, (c * PEER_HEADS + h + 1) * I_PER_EC)
            tauw_ref[j, dst, :] = tau[src, :]
            aw_ref[j, dst, :] = a[src, :]


def _peer_kernel(h1_ref, p_ref, nffn_ref, wq_ref, keys_ref, u_ref, vt_ref,
                 npg_ref, wpg_ref, wpp_ref, npo_ref, nfin_ref,
                 y_ref,
                 xn_ref, s2k_ref, bb_ref, tauw_ref, aw_ref,
                 hta_ref, htb_ref, wg_ref, acc_ref):
    e = pl.program_id(1)
    tt = xn_ref.shape[0]
    half = 2 * LANES

    def to_slabs(dst_ref, rows, val):
        for j in range(val.shape[1] // LANES):
            dst_ref[j, rows, :] = val[:, j * LANES:(j + 1) * LANES]

    def activations(dst_ref, hs):
        act = _dot_nt(u_ref[...], xn_ref[hs * half:(hs + 1) * half, :])
        dst_ref[2 * hs] = act[:, 0:LANES]
        dst_ref[2 * hs + 1] = act[:, LANES:half]

    def project(wg_ref, hs):
        j0, j1 = 2 * hs, 2 * hs + 1
        out = _dot(vt_ref[...], jnp.concatenate([wg_ref[j0], wg_ref[j1]], axis=1))
        acc_ref[j0] += out[:, 0:LANES]
        acc_ref[j1] += out[:, LANES:half]

    def gate_rows(src_ref, wg_ref, c, j):
        rows_t, rows_a = [], []
        for h in range(PEER_HEADS):
            base = pl.multiple_of((c * PEER_HEADS + h) * I_PER_EC, I_PER_EC)
            rows_t.append(tauw_ref[j, pl.ds(base, I_PER_EC), :])
            rows_a.append(aw_ref[j, pl.ds(base, I_PER_EC), :])
        for il in range(I_PER_EC):
            es = slice(il * PEER_KEYS, (il + 1) * PEER_KEYS)
            w = None
            for h in range(PEER_HEADS):
                rs = slice(h * PEER_KEYS, (h + 1) * PEER_KEYS)
                tau_b = jnp.broadcast_to(rows_t[h][il:il + 1, :], (PEER_KEYS, LANES))
                a_b = jnp.broadcast_to(rows_a[h][il:il + 1, :], (PEER_KEYS, LANES))
                term = jnp.where(s2k_ref[j, rs, :] >= tau_b, bb_ref[j, rs, :], 0.0) * a_b
                w = term if w is None else w + term
            wg_ref[j, es, :] = (w * _gelu(src_ref[j, es, :])).astype(BF16)

    def stages(c, acts_to=None, gate_from=None, gate_to=None, proj_from=None):
        for hs in range(tt // half):
            if gate_from is not None:
                gate_rows(gate_from, gate_to, c, 2 * hs)
            if acts_to is not None:
                activations(acts_to, hs)
            if gate_from is not None:
                gate_rows(gate_from, gate_to, c, 2 * hs + 1)
            if proj_from is not None:
                project(proj_from, hs)

    @pl.when(e == 0)
    def _():
        xn = _rms(h1_ref[...], nffn_ref[...]).astype(BF16)
        xn_ref[...] = xn
        s1_ref, s2_ref = htb_ref, hta_ref
        for h in range(PEER_HEADS):
            q = _dot(xn, wq_ref[:, 2 * h * LANES:2 * (h + 1) * LANES]).astype(BF16)
            for c, s_ref in ((0, s1_ref), (1, s2_ref)):
                to_slabs(s_ref, slice(h * PEER_KEYS, (h + 1) * PEER_KEYS),
                         _dot_nt(keys_ref[2 * h + c], q[:, c * LANES:(c + 1) * LANES]))

        def tile(j, carry):
            _peer_topk_tile(s1_ref, s2_ref, s2k_ref, bb_ref, tauw_ref, aw_ref, j)
            return carry
        lax.fori_loop(0, tt // LANES, tile, 0)
        acc_ref[...] = jnp.zeros_like(acc_ref)
        stages(0, acts_to=hta_ref)

    ht = (hta_ref, htb_ref)
    last = N_EC - 1

    for par in (0, 1):
        @pl.when(jnp.logical_and(jnp.logical_and(e >= 1, e <= last), e % 2 == par))
        def _():
            stages(e - 1, acts_to=ht[par], gate_from=ht[1 - par], gate_to=wg_ref, proj_from=wg_ref)

    @pl.when(e == N_EC)
    def _():
        stages(last, gate_from=ht[last % 2], gate_to=wg_ref, proj_from=wg_ref)
        peer_t = jnp.concatenate([acc_ref[j] for j in range(tt // LANES)], axis=1)
        h2 = h1_ref[...] + peer_t.T
        g = _sigmoid(_dot(_rms(h2, npg_ref[...]).astype(BF16), wpg_ref[...]))
        pe = _dot(p_ref[...].astype(BF16), wpp_ref[...])
        h3 = h2 + _rms(pe * g, npo_ref[...])
        y_ref[...] = _rms(h3, nfin_ref[...])


def _peer_ple(h1, p, w):
    t = h1.shape[0]
    tt = PEER_TT
    assert t % tt == 0
    grid = (t // tt, N_EC + 1)
    rows = PEER_HEADS * PEER_KEYS
    nl = tt // LANES
    assert rows == PEER_EC
    return pl.pallas_call(
        _peer_kernel,
        grid=grid,
        in_specs=[
            pl.BlockSpec((tt, D_MODEL), lambda i, e: (i, 0), pipeline_mode=pl.Buffered(1)),
            pl.BlockSpec((tt, PLE_DIM), lambda i, e: (i, 0), pipeline_mode=pl.Buffered(1)),
            _const_spec((1, D_MODEL)),
            _const_spec((D_MODEL, 2 * rows)),
            _const_spec((2 * PEER_HEADS, PEER_KEYS, LANES)),
            pl.BlockSpec((PEER_EC, D_MODEL), lambda i, e: (jnp.minimum(e, N_EC - 1), 0)),
            pl.BlockSpec((D_MODEL, PEER_EC), lambda i, e: (0, jnp.maximum(e - 1, 0))),
            _const_spec((1, D_MODEL)),
            _const_spec((D_MODEL, D_MODEL)),
            _const_spec((PLE_DIM, D_MODEL)),
            _const_spec((1, D_MODEL)),
            _const_spec((1, D_MODEL)),
        ],
        out_specs=pl.BlockSpec((tt, D_MODEL), lambda i, e: (i, 0)),
        out_shape=jax.ShapeDtypeStruct((t, D_MODEL), F32),
        scratch_shapes=[
            pltpu.VMEM((tt, D_MODEL), BF16),
            pltpu.VMEM((nl, rows, LANES), F32),
            pltpu.VMEM((nl, rows, LANES), F32),
            pltpu.VMEM((nl, rows, LANES), F32),
            pltpu.VMEM((nl, rows, LANES), F32),
            pltpu.VMEM((nl, PEER_EC, LANES), F32),
            pltpu.VMEM((nl, PEER_EC, LANES), F32),
            pltpu.VMEM((nl, PEER_EC, LANES), BF16),
            pltpu.VMEM((nl, D_MODEL, LANES), F32),
        ],
        compiler_params=pltpu.CompilerParams(
            dimension_semantics=("arbitrary", "arbitrary"),
            vmem_limit_bytes=VMEM_LIMIT),
        name="peer_ple",
    )(h1, p, w["norm_ffn"], w["wq"], w["keys"], w["u"], w["vt"],
      w["norm_ple_gate"], w["w_ple_gate"], w["w_ple_proj"], w["norm_ple_out"], w["norm_final"])


def _row(v):
    return v.reshape(1, -1).astype(F32)


def _prep_peer_weights(a):
    return {
        "norm_ffn": _row(a["norm_ffn"][0]),
        "wq": a["peer_wq"][0].astype(BF16),
        "keys": a["peer_keys"][0].reshape(2 * PEER_HEADS, PEER_KEYS, LANES).astype(BF16),
        "u": a["peer_u"][0].astype(BF16),
        "vt": a["peer_v"][0].astype(BF16).T,
        "norm_ple_gate": _row(a["norm_ple_gate"][0]),
        "w_ple_gate": a["w_ple_gate"][0].astype(BF16),
        "w_ple_proj": a["w_ple_proj"][0].astype(BF16),
        "norm_ple_out": _row(a["norm_ple_out"][0]),
        "norm_final": _row(a["norm_final"]),
    }


CHUNK = 128
MIX_TM = 256
HIST = SUBLANES


def _conv_from_ext(ext_ref, w, b, n):
    ext = ext_ref[0:HIST + n, :]
    y = b + ext[HIST:HIST + n, :] * w[CONV_W - 1:CONV_W, :]
    for d in range(1, CONV_W):
        y = y + pltpu.roll(ext, d, 0)[HIST:HIST + n, :] * w[CONV_W - 1 - d:CONV_W - d, :]
    return y


def _lru_coeffs(xl, gates):
    wr_ref, br_ref, wi_ref, bi_ref, lam_ref = gates
    xlb = xl.astype(BF16)

    def gate_dot(w_ref):
        return jnp.concatenate(
            [_dot(xlb[:, k * MXU_DIM:(k + 1) * MXU_DIM], w_ref[k]) for k in range(w_ref.shape[0])], axis=1)

    r = _sigmoid(gate_dot(wr_ref) + br_ref[...])
    ig = _sigmoid(gate_dot(wi_ref) + bi_ref[...])
    log_a = -LRU_C * r * _softplus(-lam_ref[...])
    a_t = jnp.exp(log_a)
    mult = jnp.sqrt(-_expm1(2.0 * log_a))
    return a_t, mult * (ig * xl)


def _ssd_chunk(xbc, dt, a, st_ref, consts, dsk):
    tri_ref, bdmask_ref = consts
    tri = tri_ref[...]
    da = dt * a
    cs = _dot3_left(tri, da)
    lane = lax.broadcasted_iota(jnp.int32, (CHUNK, LANES), 1)
    row = lax.broadcasted_iota(jnp.int32, (CHUNK, LANES), 0)
    causal = row >= lane
    low = lane < SSD_HEAD_DIM

    def per_channel(v):
        cols = [jnp.broadcast_to(v[:, h:h + 1], (CHUNK, LANES)) for h in range(SSD_HEADS)]
        wide = jnp.concatenate([jnp.where(low, cols[2 * k], cols[2 * k + 1]) for k in range(SSD_HEADS // 2)], axis=1)
        return cols, wide

    cs_cols, cs_e = per_channel(cs)
    _, dt_e = per_channel(dt)
    xs = xbc[:, 0:SSD_WIDTH]
    xdt = xs * dt_e
    bmat = xbc[:, SSD_WIDTH:SSD_WIDTH + LANES]
    cmat = xbc[:, SSD_WIDTH + LANES:SSD_WIDTH + 2 * LANES]
    last_e = cs_e[CHUNK - 1:CHUNK, :]
    xd = (xdt * jnp.exp(last_e - cs_e)).astype(BF16)
    st = st_ref[...]
    y_off = _dot(cmat.astype(BF16), st.astype(BF16)) * jnp.exp(cs_e)
    st_ref[...] = st * jnp.exp(last_e) + _dot(bmat.T.astype(BF16), xd) * bdmask_ref[...]

    cs_t = cs.T
    bmat_b = bmat.astype(BF16)
    cb = [_dot_nt(jnp.where(low, cmat, 0.0).astype(BF16), bmat_b),
          _dot_nt(jnp.where(low, 0.0, cmat).astype(BF16), bmat_b)]
    xdt_b = xdt.astype(BF16)
    pieces = []
    for hp in range(SSD_HEADS // 2):
        cols = slice(hp * LANES, (hp + 1) * LANES)
        outs = []
        for h in (2 * hp, 2 * hp + 1):
            seg = jnp.exp(cs_cols[h] - cs_t[h:h + 1, :])
            m = (cb[h // (SSD_HEADS // SSD_GROUPS)] * jnp.where(causal, seg, 0.0)).astype(BF16)
            outs.append(_dot(m, xdt_b[:, cols]))
        pieces.append(jnp.where(low, outs[0], outs[1]))
    y_diag = jnp.concatenate(pieces, axis=1)
    return y_diag + y_off + dsk * xs


def _mixer_out(x, y, z, y_lru, nssd_ref, wout_ref):
    y_ssd = _rms(y * _silu(z), nssd_ref[...])
    return (x + _dot(y_ssd.astype(BF16), wout_ref[0:SSD_WIDTH, :])
            + _dot(y_lru.astype(BF16), wout_ref[SSD_WIDTH:SSD_WIDTH + LRU_WIDTH, :]))


def _mix_p_kernel(x_ref, nmix_ref, wall_ref, cws_ref, cbs_ref, dtb_ref, alog_ref, dsk_ref, nssd_ref,
                  cwl_ref, cbl_ref, wr_ref, br_ref, wi_ref, bi_ref, lam_ref, nlru_ref, wout_ref,
                  tri_ref, bdmask_ref,
                  h1_ref, st_out_ref, sconv_out_ref, lstate_out_ref, lconv_out_ref,
                  exts_ref, extl_ref, st_ref, hcar_ref, y_ref, pa_ref, hb_ref):
    l = pl.program_id(1)
    tm = x_ref.shape[0]

    @pl.when(l == 0)
    def _():
        exts_ref[0:HIST, :] = jnp.zeros((HIST, SSD_CONV_DIM), F32)
        extl_ref[0:HIST, :] = jnp.zeros((HIST, LRU_WIDTH), F32)
        st_ref[...] = jnp.zeros_like(st_ref)
        hcar_ref[...] = jnp.zeros_like(hcar_ref)

    x = x_ref[...]
    xn = _rms(x, nmix_ref[...]).astype(BF16)

    def proj(lo, hi):
        return _dot(xn, wall_ref[:, lo:hi])

    exts_ref[HIST:HIST + tm, :] = proj(P_XBC, P_DT)
    xbc = _silu(_conv_from_ext(exts_ref, cws_ref[...], cbs_ref[...], tm))
    exts_ref[0:HIST, :] = exts_ref[tm:tm + HIST, :]
    dt = _softplus(proj(P_DT, P_GATE) + dtb_ref[...])
    a = -jnp.exp(alog_ref[...])
    consts = (tri_ref, bdmask_ref)
    extl_ref[HIST:HIST + tm, :] = proj(P_XL, P_ALL)
    for c in range(tm // CHUNK):
        rs = slice(c * CHUNK, (c + 1) * CHUNK)
        y_ref[rs, :] = _ssd_chunk(xbc[rs, :], dt[rs, :], a, st_ref, consts, dsk_ref[...])
    z = proj(P_Z, P_XBC)
    gate = proj(P_GATE, P_XL)

    xl = _conv_from_ext(extl_ref, cwl_ref[...], cbl_ref[...], tm)
    extl_ref[0:HIST, :] = extl_ref[tm:tm + HIST, :]
    a_t, bx = _lru_coeffs(xl, (wr_ref, br_ref, wi_ref, bi_ref, lam_ref))
    seg = tm // SUBLANES
    pitch = seg + SUBLANES
    hs_cols = []
    for j in range(LRU_WIDTH // LANES):
        cols = slice(j * LANES, (j + 1) * LANES)
        for s in range(SUBLANES):
            pa_ref[j, s * pitch:s * pitch + seg, :] = a_t[s * seg:(s + 1) * seg, cols]
            hb_ref[j, s * pitch:s * pitch + seg, :] = bx[s * seg:(s + 1) * seg, cols]
        hloc = jnp.zeros((SUBLANES, LANES), F32)
        ploc = jnp.ones((SUBLANES, LANES), F32)
        for tau in range(seg):
            rows = pl.ds(tau, SUBLANES, stride=pitch)
            a_r = pa_ref[j, rows, :]
            hloc = a_r * hloc + hb_ref[j, rows, :]
            ploc = a_r * ploc
            hb_ref[j, rows, :] = hloc
            pa_ref[j, rows, :] = ploc
        carry = hcar_ref[0:1, cols]
        parts = []
        for s in range(SUBLANES):
            rs = slice(s * pitch, s * pitch + seg)
            parts.append(hb_ref[j, rs, :] + pa_ref[j, rs, :] * carry)
            carry = hloc[s:s + 1, :] + ploc[s:s + 1, :] * carry
        hcar_ref[0:1, cols] = carry
        hs_cols.append(jnp.concatenate(parts, axis=0))
    hs = jnp.concatenate(hs_cols, axis=1)
    y_lru = _rms(hs * _gelu(gate), nlru_ref[...])

    h1_ref[...] = _mixer_out(x, y_ref[...], z, y_lru, nssd_ref, wout_ref)

    @pl.when(l == pl.num_programs(1) - 1)
    def _():
        st_out_ref[0] = st_ref[...]
        sconv_out_ref[0] = exts_ref[0:HIST, :]
        lstate_out_ref[0] = hcar_ref[...]
        lconv_out_ref[0] = extl_ref[0:HIST, :]


def _mix_prompt(x, w):
    b, l, _ = x.shape
    tm = MIX_TM
    assert l % tm == 0
    nl = l // tm
    names = ["norm_mix", "w_all", "conv_ssd_w", "conv_ssd_b", "dt_bias", "a_log", "d_skip", "norm_ssd",
             "conv_lru_w", "conv_lru_b", "w_r", "b_r", "w_i", "b_i", "lam", "norm_lru", "w_out",
             "tri", "bdmask"]
    ws = [w[n] for n in names]
    gs = SSD_GROUPS * SSD_STATE
    out_shape = (
        jax.ShapeDtypeStruct((b * l, D_MODEL), F32),
        jax.ShapeDtypeStruct((b, gs, SSD_WIDTH), F32),
        jax.ShapeDtypeStruct((b, HIST, SSD_CONV_DIM), F32),
        jax.ShapeDtypeStruct((b, SUBLANES, LRU_WIDTH), F32),
        jax.ShapeDtypeStruct((b, HIST, LRU_WIDTH), F32),
    )
    per_b = lambda shape: pl.BlockSpec((1,) + shape, lambda i, j: (i, 0, 0))
    return pl.pallas_call(
        _mix_p_kernel,
        grid=(b, nl),
        in_specs=[pl.BlockSpec((tm, D_MODEL), lambda i, j: (i * nl + j, 0))]
        + [_const_spec(v.shape) for v in ws],
        out_specs=(
            pl.BlockSpec((tm, D_MODEL), lambda i, j: (i * nl + j, 0)),
            per_b((gs, SSD_WIDTH)), per_b((HIST, SSD_CONV_DIM)),
            per_b((SUBLANES, LRU_WIDTH)), per_b((HIST, LRU_WIDTH)),
        ),
        out_shape=out_shape,
        scratch_shapes=[
            pltpu.VMEM((HIST + tm, SSD_CONV_DIM), F32),
            pltpu.VMEM((HIST + tm, LRU_WIDTH), F32),
            pltpu.VMEM((gs, SSD_WIDTH), F32),
            pltpu.VMEM((SUBLANES, LRU_WIDTH), F32),
            pltpu.VMEM((tm, SSD_WIDTH), F32),
            pltpu.VMEM((LRU_WIDTH // LANES, tm + SUBLANES * SUBLANES, LANES), F32),
            pltpu.VMEM((LRU_WIDTH // LANES, tm + SUBLANES * SUBLANES, LANES), F32),
        ],
        compiler_params=pltpu.CompilerParams(
            dimension_semantics=("arbitrary", "arbitrary"),
            vmem_limit_bytes=VMEM_LIMIT),
        name="mix_prompt",
    )(x.reshape(b * l, D_MODEL), *ws)


NS = 4
NB = 128


def _mix_s1_kernel(x_ref, bufs_ref, bufl_ref, lst_ref, nmix_ref, wall_ref, cws_ref, cbs_ref, dtb_ref,
                   cwl_ref, cbl_ref, wr_ref, br_ref, wi_ref, bi_ref, lam_ref, nlru_ref,
                   z_ref, xst_ref, bct_ref, dtt_ref, ylru_ref, sconv_ref, lstate_ref, lconv_ref):
    xn = _rms(x_ref[...], nmix_ref[...]).astype(BF16)
    proj = _dot(xn, wall_ref[...])
    z_ref[...] = proj[:, P_Z:P_XBC]

    def blocks(buf_ref, raw):
        return ([buf_ref[j * NB:(j + 1) * NB, :] for j in range(CONV_W - 1)]
                + [raw[t * NB:(t + 1) * NB, :] for t in range(NS)])

    def conv(xp, w, b, t):
        y = b + xp[t] * w[0:1, :]
        for k in range(1, CONV_W):
            y = y + xp[t + k] * w[k:k + 1, :]
        return y

    xp = blocks(bufs_ref, proj[:, P_XBC:P_DT])
    for j in range(CONV_W - 1):
        sconv_ref[j * NB:(j + 1) * NB, :] = xp[NS + j]
    for t in range(NS):
        xbc = _silu(conv(xp, cws_ref[...], cbs_ref[...], t))
        dt = _softplus(proj[t * NB:(t + 1) * NB, P_DT:P_GATE] + dtb_ref[...])
        xst_ref[t] = xbc[:, 0:SSD_WIDTH].T
        bct_ref[t] = xbc[:, SSD_WIDTH:SSD_CONV_DIM].T
        dtt_ref[t] = dt.T

    xp = blocks(bufl_ref, proj[:, P_XL:P_ALL])
    for j in range(CONV_W - 1):
        lconv_ref[j * NB:(j + 1) * NB, :] = xp[NS + j]
    h = lst_ref[...]
    for t in range(NS):
        rs = slice(t * NB, (t + 1) * NB)
        xl = conv(xp, cwl_ref[...], cbl_ref[...], t)
        a_t, bx = _lru_coeffs(xl, (wr_ref, br_ref, wi_ref, bi_ref, lam_ref))
        h = a_t * h + bx
        ylru_ref[rs, :] = _rms(h * _gelu(proj[rs, P_GATE:P_XL]), nlru_ref[...])
    lstate_ref[...] = h


def _mix_s1(x_tm, bufs_tm, bufl_tm, lstate, w):
    n = NS * NB
    names = ["norm_mix", "w_all", "conv_ssd_w", "conv_ssd_b", "dt_bias",
             "conv_lru_w", "conv_lru_b", "w_r", "b_r", "w_i", "b_i", "lam", "norm_lru"]
    out_shape = (
        jax.ShapeDtypeStruct((n, SSD_WIDTH), F32),
        jax.ShapeDtypeStruct((NS, SSD_WIDTH, NB), F32),
        jax.ShapeDtypeStruct((NS, SSD_CONV_DIM - SSD_WIDTH, NB), F32),
        jax.ShapeDtypeStruct((NS, LANES, NB), F32),
        jax.ShapeDtypeStruct((n, LRU_WIDTH), F32),
        jax.ShapeDtypeStruct(((CONV_W - 1) * NB, SSD_CONV_DIM), F32),
        jax.ShapeDtypeStruct((NB, LRU_WIDTH), F32),
        jax.ShapeDtypeStruct(((CONV_W - 1) * NB, LRU_WIDTH), F32),
    )
    return pl.pallas_call(
        _mix_s1_kernel, out_shape=out_shape,
        compiler_params=pltpu.CompilerParams(vmem_limit_bytes=VMEM_LIMIT),
        name="mix_sample_in",
    )(x_tm, bufs_tm, bufl_tm, lstate, *[w[k] for k in names])


def _ssd_s_kernel(xs_ref, bc_ref, dt_ref, alog_ref, dsk_ref, s_ref, y_ref, ns_ref):
    h = pl.program_id(0)
    g = h // (SSD_HEADS // SSD_GROUPS)
    b_rows = pl.ds(pl.multiple_of(g * SSD_STATE, SSD_STATE), SSD_STATE)
    c_rows = pl.ds(pl.multiple_of(LANES + g * SSD_STATE, SSD_STATE), SSD_STATE)
    a = -jnp.exp(alog_ref[0])
    ns_ref[...] = s_ref[...]
    for t in range(NS):
        dtr = dt_ref[t, 0]
        decay = jnp.exp(dtr * a)
        bt = bc_ref[t, b_rows, :]
        ct = bc_ref[t, c_rows, :]
        xs = xs_ref[t]
        xdt = xs * dtr
        for p in range(SSD_HEAD_DIM):
            s_p = decay * ns_ref[0, p] + xdt[p:p + 1, :] * bt
            ns_ref[0, p] = s_p
            y_ref[t, p:p + 1, :] = jnp.sum(ct * s_p, axis=0, keepdims=True)
        y_ref[t] = y_ref[t] + dsk_ref[0] * xs


def _ssd_sample(xst, bct, dtt, alog_b, dsk_b, state_bm):
    return pl.pallas_call(
        _ssd_s_kernel,
        grid=(SSD_HEADS,),
        in_specs=[
            pl.BlockSpec((NS, SSD_HEAD_DIM, NB), lambda h: (0, h, 0)),
            pl.BlockSpec((NS, SSD_CONV_DIM - SSD_WIDTH, NB), lambda h: (0, 0, 0)),
            pl.BlockSpec((NS, 1, 1, NB), lambda h: (0, h, 0, 0)),
            pl.BlockSpec((1, 1, NB), lambda h: (h, 0, 0)),
            pl.BlockSpec((1, 1, NB), lambda h: (h, 0, 0)),
            pl.BlockSpec((1, SSD_HEAD_DIM, SSD_STATE, NB), lambda h: (h, 0, 0, 0)),
        ],
        out_specs=(
            pl.BlockSpec((NS, SSD_HEAD_DIM, NB), lambda h: (0, h, 0)),
            pl.BlockSpec((1, SSD_HEAD_DIM, SSD_STATE, NB), lambda h: (h, 0, 0, 0)),
        ),
        out_shape=(
            jax.ShapeDtypeStruct((NS, SSD_WIDTH, NB), F32),
            jax.ShapeDtypeStruct((SSD_HEADS, SSD_HEAD_DIM, SSD_STATE, NB), F32),
        ),
        compiler_params=pltpu.CompilerParams(dimension_semantics=("arbitrary",),
                                             vmem_limit_bytes=VMEM_LIMIT),
        name="ssd_sample",
    )(xst, bct, dtt, alog_b, dsk_b, state_bm)


def _mix_s3_kernel(x_ref, yt_ref, z_ref, ylru_ref, nssd_ref, wout_ref, h1_ref):
    for t in range(NS):
        rs = slice(t * NB, (t + 1) * NB)
        h1_ref[rs, :] = _mixer_out(x_ref[rs, :], yt_ref[t].T, z_ref[rs, :], ylru_ref[rs, :],
                                   nssd_ref, wout_ref)


def _mix_s3(x_tm, yt, z, ylru, w):
    return pl.pallas_call(
        _mix_s3_kernel,
        out_shape=jax.ShapeDtypeStruct((NS * NB, D_MODEL), F32),
        compiler_params=pltpu.CompilerParams(vmem_limit_bytes=VMEM_LIMIT),
        name="mix_sample_out",
    )(x_tm, yt, z, ylru, w["norm_ssd"], w["w_out"])


def _pad_lanes(v):
    return jnp.pad(v.reshape(1, -1).astype(F32), ((0, 0), (0, LANES - v.size)))


def _gate_tiles(wb):
    k, n, _ = wb.shape
    per = MXU_DIM // n
    return jnp.stack([_block_diag(wb[c * per:(c + 1) * per]) for c in range(k // per)]).astype(BF16)


def _block_diag(wb):
    k, n, _ = wb.shape
    eye = jnp.eye(k, dtype=wb.dtype)
    return (eye[:, None, :, None] * wb[:, :, None, :]).reshape(k * n, k * n)


def _prep_mixer_weights(a):
    w_in = a["w_in"][0]
    s1 = SSD_WIDTH
    s2 = s1 + SSD_CONV_DIM
    s3 = s2 + SSD_HEADS
    s4 = s3 + LRU_WIDTH
    pad = jnp.zeros((D_MODEL, LANES - SSD_HEADS), w_in.dtype)
    w_all = jnp.concatenate([w_in[:, :s2], w_in[:, s2:s3], pad, w_in[:, s3:]], axis=1).astype(BF16)
    hd = jnp.arange(SSD_WIDTH) // SSD_HEAD_DIM
    grp_rows = jnp.arange(SSD_GROUPS * SSD_STATE) // SSD_STATE
    grp_cols = hd // (SSD_HEADS // SSD_GROUPS)
    bdmask = (grp_rows[:, None] == grp_cols[None, :]).astype(F32)
    tri = (jnp.arange(CHUNK)[:, None] >= jnp.arange(CHUNK)[None, :]).astype(BF16)
    return {
        "norm_mix": _row(a["norm_mix"][0]),
        "w_all": w_all,
        "conv_ssd_w": a["conv_ssd_w"][0].astype(F32),
        "conv_ssd_b": _row(a["conv_ssd_b"][0]),
        "dt_bias": _pad_lanes(a["dt_bias"][0]),
        "a_log": _pad_lanes(a["a_log"][0]),
        "d_skip": _row(jnp.repeat(a["d_skip"][0], SSD_HEAD_DIM)),
        "norm_ssd": _row(a["norm_ssd"][0]),
        "conv_lru_w": a["conv_lru_w"][0].astype(F32),
        "conv_lru_b": _row(a["conv_lru_b"][0]),
        "w_r": _gate_tiles(a["w_rgate"][0]),
        "b_r": _row(a["b_rgate"][0]),
        "w_i": _gate_tiles(a["w_igate"][0]),
        "b_i": _row(a["b_igate"][0]),
        "lam": _row(a["lru_lambda"][0]),
        "norm_lru": _row(a["norm_lru"][0]),
        "w_out": a["w_out"][0].astype(BF16),
        "tri": tri, "bdmask": bdmask,
        "a_log_b": jnp.broadcast_to(a["a_log"][0].astype(F32)[:, None, None], (SSD_HEADS, 1, NB)),
        "d_skip_b": jnp.broadcast_to(a["d_skip"][0].astype(F32)[:, None, None], (SSD_HEADS, 1, NB)),
    }


def _time_major(v):
    return jnp.swapaxes(v, 0, 1).reshape(-1, v.shape[-1])


def _seq_major(v, n):
    return jnp.swapaxes(v.reshape(n, NB, v.shape[-1]), 0, 1)


def _layer_prompt(x, p, wm, wp):
    b = x.shape[0]
    h1, st, sconv, lstate, lconv = _mix_prompt(x, wm)
    y = _peer_ple(h1, p.reshape(-1, PLE_DIM), wp).reshape(x.shape)
    st5 = st.reshape(b, SSD_GROUPS, SSD_STATE, SSD_HEADS, SSD_HEAD_DIM)
    hpg = SSD_HEADS // SSD_GROUPS
    ssd_state = jnp.stack([st5[:, h // hpg, :, h, :] for h in range(SSD_HEADS)], axis=1)
    ssd_state = jnp.swapaxes(ssd_state, 2, 3)
    keep = slice(HIST - (CONV_W - 1), HIST)
    return y, ssd_state, sconv[:, keep, :], lstate[:, 0, :], lconv[:, keep, :]


def _layer_sample(x, p, s_ssd, s_sconv, s_lru, s_lconv, wm, wp):
    x_tm = _time_major(x)
    z, xst, bct, dtt, ylru, sconv, lstate, lconv = _mix_s1(
        x_tm, _time_major(s_sconv), _time_major(s_lconv), s_lru, wm)
    dtt = dtt[:, :SSD_HEADS, :].reshape(NS, SSD_HEADS, 1, NB)
    state_bm = jnp.transpose(s_ssd, (1, 2, 3, 0))
    yt, new_bm = _ssd_sample(xst, bct, dtt, wm["a_log_b"], wm["d_skip_b"], state_bm)
    h1 = _mix_s3(x_tm, yt, z, ylru, wm)
    y = _seq_major(_peer_ple(h1, _time_major(p), wp), NS)
    ssd_state = jnp.transpose(new_bm, (3, 0, 1, 2))
    return y, ssd_state, _seq_major(sconv, CONV_W - 1), lstate, _seq_major(lconv, CONV_W - 1)


def kernel(x_prompt, x_sample, state_ssd, state_ssd_conv, state_lru, state_lru_conv, p_prompt, p_sample, norm_mix, w_in, conv_ssd_w, conv_ssd_b, dt_bias, a_log, d_skip, norm_ssd, conv_lru_w, conv_lru_b, w_rgate, b_rgate, w_igate, b_igate, lru_lambda, norm_lru, w_out, norm_ffn, peer_wq, peer_keys, peer_u, peer_v, norm_ple_gate, w_ple_gate, w_ple_proj, norm_ple_out, norm_final):
    assert x_sample.shape[:2] == (NB, NS) and norm_mix.shape[0] == 1
    a = dict(norm_mix=norm_mix, w_in=w_in, conv_ssd_w=conv_ssd_w, conv_ssd_b=conv_ssd_b, dt_bias=dt_bias,
             a_log=a_log, d_skip=d_skip, norm_ssd=norm_ssd, conv_lru_w=conv_lru_w, conv_lru_b=conv_lru_b,
             w_rgate=w_rgate, b_rgate=b_rgate, w_igate=w_igate, b_igate=b_igate, lru_lambda=lru_lambda,
             norm_lru=norm_lru, w_out=w_out, norm_ffn=norm_ffn, peer_wq=peer_wq, peer_keys=peer_keys,
             peer_u=peer_u, peer_v=peer_v, norm_ple_gate=norm_ple_gate, w_ple_gate=w_ple_gate,
             w_ple_proj=w_ple_proj, norm_ple_out=norm_ple_out, norm_final=norm_final)
    wm = _prep_mixer_weights(a)
    wp = _prep_peer_weights(a)
    yp, p_ssd, p_sconv, p_lru, p_lconv = _layer_prompt(x_prompt, p_prompt[0], wm, wp)
    ys, s_ssd, s_sconv, s_lru, s_lconv = _layer_sample(
        x_sample, p_sample[0], state_ssd[0], state_ssd_conv[0], state_lru[0], state_lru_conv[0], wm, wp)
    return (yp, ys, p_ssd[None], p_sconv[None], p_lru[None], p_lconv[None],
            s_ssd[None], s_sconv[None], s_lru[None], s_lconv[None])
```

```python
import math

import jax
import jax.numpy as jnp
from jax import lax
from jax.experimental import pallas as pl
from jax.experimental.pallas import tpu as pltpu

F32 = jnp.float32
BF16 = jnp.bfloat16

D_MODEL = 1024
SSD_WIDTH = 1024
SSD_HEADS = 16
SSD_HEAD_DIM = 64
SSD_STATE = 64
SSD_GROUPS = 2
SSD_CONV_DIM = 1280
LRU_WIDTH = 1024
LRU_BLOCKS = 16
LRU_C = 8.0
CONV_W = 4
PEER_HEADS = 8
PEER_KEYS = 128
PEER_EXPERTS = PEER_KEYS * PEER_KEYS
PEER_TOPK = 16
PLE_DIM = 256
EPS = 1e-6

LANES = 128
SUBLANES = 8
MXU_DIM = 256
VMEM_LIMIT = 60 * 1024 * 1024

P_Z = 0
P_XBC = P_Z + SSD_WIDTH
P_DT = P_XBC + SSD_CONV_DIM
P_GATE = P_DT + LANES
P_XL = P_GATE + LRU_WIDTH
P_ALL = P_XL + LRU_WIDTH


def _rms(x, g):
    return x * lax.rsqrt(jnp.mean(x * x, axis=-1, keepdims=True) + EPS) * g


def _sigmoid(x):
    return 1.0 / (1.0 + jnp.exp(-x))


def _silu(x):
    return x * _sigmoid(x)


def _gelu(x):
    k1 = -2.0 * math.sqrt(2.0 / math.pi) * math.log2(math.e)
    k3 = k1 * 0.044715
    return x / (1.0 + jnp.exp2(x * (k1 + k3 * (x * x))))


def _softplus(x):
    return jnp.maximum(x, 0.0) + jnp.log1p(jnp.exp(-jnp.abs(x)))


def _expm1(x):
    u = jnp.exp(x)
    near = (u - 1.0) * x / jnp.log(u)
    return jnp.where(u == 1.0, x, jnp.where(x < -1.0, u - 1.0, near))


def _dot(a, b):
    return jnp.dot(a, b, preferred_element_type=F32)


def _dot_nt(a, b):
    return lax.dot_general(a, b, (((1,), (1,)), ((), ())), preferred_element_type=F32)


def _split3(x):
    hi = x.astype(BF16)
    r = x - hi.astype(F32)
    mid = r.astype(BF16)
    lo = (r - mid.astype(F32)).astype(BF16)
    return hi, mid, lo


def _dot3_left(m, x):
    hi, mid, lo = _split3(x)
    return _dot(m, hi) + _dot(m, mid) + _dot(m, lo)


def _const_spec(shape):
    nd = len(shape)
    return pl.BlockSpec(shape, lambda *_: (0,) * nd, pipeline_mode=pl.Buffered(1))


PEER_TT = 512
PEER_EC = 1024
N_EC = PEER_EXPERTS // PEER_EC
I_PER_EC = PEER_EC // PEER_KEYS


def _batcher_network(lo, hi):
    def merge(lo, hi, r):
        step = 2 * r
        if step < hi - lo:
            yield from merge(lo, hi, step)
            yield from merge(lo + r, hi, step)
            yield from ((i, i + r) for i in range(lo + r, hi - r, step))
        else:
            yield (lo, lo + r)
    if hi - lo >= 1:
        mid = lo + (hi - lo) // 2
        yield from _batcher_network(lo, mid)
        yield from _batcher_network(mid + 1, hi)
        yield from merge(lo, hi, 1)


SORT16 = tuple(_batcher_network(0, PEER_TOPK - 1))
assert PEER_KEYS == PEER_TOPK * SUBLANES


def _cmpx(v, a, b):
    v[a], v[b] = jnp.maximum(v[a], v[b]), jnp.minimum(v[a], v[b])


def _top16_desc(tiles):
    v = list(tiles)
    for a, b in SORT16:
        _cmpx(v, a, b)
    for shift in (4, 2, 1):
        w = [pltpu.roll(x, shift, 0) for x in v]
        v = [jnp.maximum(v[k], w[PEER_TOPK - 1 - k]) for k in range(PEER_TOPK)]
        for d in (8, 4, 2, 1):
            for k in range(PEER_TOPK):
                if not k & d:
                    _cmpx(v, k, k + d)
    return v


def _peer_topk_tile(s1_ref, s2_ref, s2k_ref, bb_ref, tauw_ref, aw_ref, j):
    ninf = jnp.full((SUBLANES, LANES), -jnp.inf, F32)
    sub = lax.broadcasted_iota(jnp.int32, (SUBLANES, LANES), 0)

    def rows_of(vals):
        out = vals[SUBLANES - 1]
        for r in range(SUBLANES - 2, -1, -1):
            out = jnp.where(sub == r, vals[r], out)
        return out

    for h in range(PEER_HEADS):
        rs = slice(h * PEER_KEYS, (h + 1) * PEER_KEYS)
        s1 = s1_ref[j, rs, :]
        s2 = s2_ref[j, rs, :]
        t1 = _top16_desc([s1[k * SUBLANES:(k + 1) * SUBLANES, :] for k in range(PEER_TOPK)])
        t2 = _top16_desc([s2[k * SUBLANES:(k + 1) * SUBLANES, :] for k in range(PEER_TOPK)])
        t2_lo, t2_hi, t1_hi = rows_of(t2[0:8]), rows_of(t2[8:16]), rows_of(t1[8:16])
        cand = [t1[0] + t2_lo, t1[0] + t2_hi] + [t1[k1] + t2_lo for k1 in range(1, 8)] + [t1_hi + t2[0]]
        best = _top16_desc(cand + [ninf] * (PEER_TOPK - len(cand)))
        thr = best[PEER_TOPK - 1]
        z = jnp.zeros((SUBLANES, LANES), F32)
        for k in range(PEER_TOPK):
            z = z + jnp.exp(best[k] - best[0])
        inv_z = 1.0 / z
        tau = jnp.full((PEER_KEYS, LANES), jnp.inf, F32)
        for r2 in range(PEER_TOPK):
            t2_b = jnp.concatenate([t2[r2]] * PEER_TOPK, axis=0)
            tau = jnp.where(s1 + t2_b >= jnp.concatenate([thr] * PEER_TOPK, axis=0), t2_b, tau)
        a = jnp.exp(s1 - jnp.concatenate([t1[0]] * PEER_TOPK, axis=0)) * jnp.concatenate([inv_z] * PEER_TOPK, axis=0)
        s2k_ref[j, rs, :] = s2
        bb_ref[j, rs, :] = jnp.exp(s2 - jnp.concatenate([t2[0]] * PEER_TOPK, axis=0))
        for c in range(N_EC):
            src = slice(c * I_PER_EC, (c + 1) * I_PER_EC)
            dst = slice((c * PEER_HEADS + h) * I_PER_EC, (c * PEER_HEADS + h + 1) * I_PER_EC)
            tauw_ref[j, dst, :] = tau[src, :]
            aw_ref[j, dst, :] = a[src, :]


def _peer_kernel(h1_ref, p_ref, nffn_ref, wq_ref, keys_ref, u_ref, vt_ref,
                 npg_ref, wpg_ref, wpp_ref, npo_ref, nfin_ref,
                 y_ref,
                 xn_ref, s2k_ref, bb_ref, tauw_ref, aw_ref,
                 hta_ref, htb_ref, wg_ref, acc_ref):
    e = pl.program_id(1)
    tt = xn_ref.shape[0]
    half = 2 * LANES

    def to_slabs(dst_ref, rows, val):
        for j in range(val.shape[1] // LANES):
            dst_ref[j, rows, :] = val[:, j * LANES:(j + 1) * LANES]

    def activations(dst_ref, hs):
        act = _dot_nt(u_ref[...], xn_ref[hs * half:(hs + 1) * half, :])
        dst_ref[2 * hs] = act[:, 0:LANES]
        dst_ref[2 * hs + 1] = act[:, LANES:half]

    def project(wg_ref, hs):
        j0, j1 = 2 * hs, 2 * hs + 1
        out = _dot(vt_ref[...], jnp.concatenate([wg_ref[j0], wg_ref[j1]], axis=1))
        acc_ref[j0] += out[:, 0:LANES]
        acc_ref[j1] += out[:, LANES:half]

    def gate_rows(src_ref, wg_ref, c, j):
        rows_t, rows_a = [], []
        for h in range(PEER_HEADS):
            base = pl.multiple_of((c * PEER_HEADS + h) * I_PER_EC, I_PER_EC)
            rows_t.append(tauw_ref[j, pl.ds(base, I_PER_EC), :])
            rows_a.append(aw_ref[j, pl.ds(base, I_PER_EC), :])
        for il in range(I_PER_EC):
            es = slice(il * PEER_KEYS, (il + 1) * PEER_KEYS)
            w = None
            for h in range(PEER_HEADS):
                rs = slice(h * PEER_KEYS, (h + 1) * PEER_KEYS)
                tau_b = jnp.broadcast_to(rows_t[h][il:il + 1, :], (PEER_KEYS, LANES))
                a_b = jnp.broadcast_to(rows_a[h][il:il + 1, :], (PEER_KEYS, LANES))
                term = jnp.where(s2k_ref[j, rs, :] >= tau_b, bb_ref[j, rs, :], 0.0) * a_b
                w = term if w is None else w + term
            wg_ref[j, es, :] = (w * _gelu(src_ref[j, es, :])).astype(BF16)

    def stages(c, acts_to=None, gate_from=None, gate_to=None, proj_from=None):
        for hs in range(tt // half):
            if gate_from is not None:
                gate_rows(gate_from, gate_to, c, 2 * hs)
            if acts_to is not None:
                activations(acts_to, hs)
            if gate_from is not None:
                gate_rows(gate_from, gate_to, c, 2 * hs + 1)
            if proj_from is not None:
                project(proj_from, hs)

    @pl.when(e == 0)
    def _():
        xn = _rms(h1_ref[...], nffn_ref[...]).astype(BF16)
        xn_ref[...] = xn
        s1_ref, s2_ref = htb_ref, hta_ref
        for h in range(PEER_HEADS):
            q = _dot(xn, wq_ref[:, 2 * h * LANES:2 * (h + 1) * LANES]).astype(BF16)
            for c, s_ref in ((0, s1_ref), (1, s2_ref)):
                to_slabs(s_ref, slice(h * PEER_KEYS, (h + 1) * PEER_KEYS),
                         _dot_nt(keys_ref[2 * h + c], q[:, c * LANES:(c + 1) * LANES]))

        def tile(j, carry):
            _peer_topk_tile(s1_ref, s2_ref, s2k_ref, bb_ref, tauw_ref, aw_ref, j)
            return carry
        lax.fori_loop(0, tt // LANES, tile, 0)
        acc_ref[...] = jnp.zeros_like(acc_ref)
        stages(0, acts_to=hta_ref)

    ht = (hta_ref, htb_ref)
    last = N_EC - 1

    for par in (0, 1):
        @pl.when(jnp.logical_and(jnp.logical_and(e >= 1, e <= last), e % 2 == par))
        def _():
            stages(e - 1, acts_to=ht[par], gate_from=ht[1 - par], gate_to=wg_ref, proj_from=wg_ref)

    @pl.when(e == N_EC)
    def _():
        stages(last, gate_from=ht[last % 2], gate_to=wg_ref, proj_from=wg_ref)
        peer_t = jnp.concatenate([acc_ref[j] for j in range(tt // LANES)], axis=1)
        h2 = h1_ref[...] + peer_t.T
        g = _sigmoid(_dot(_rms(h2, npg_ref[...]).astype(BF16), wpg_ref[...]))
        pe = _dot(p_ref[...].astype(BF16), wpp_ref[...])
        h3 = h2 + _rms(pe * g, npo_ref[...])
        y_ref[...] = _rms(h3, nfin_ref[...])


def _peer_ple(h1, p, w):
    t = h1.shape[0]
    tt = PEER_TT
    assert t % tt == 0
    grid = (t // tt, N_EC + 1)
    rows = PEER_HEADS * PEER_KEYS
    nl = tt // LANES
    assert rows == PEER_EC
    return pl.pallas_call(
        _peer_kernel,
        grid=grid,
        in_specs=[
            pl.BlockSpec((tt, D_MODEL), lambda i, e: (i, 0), pipeline_mode=pl.Buffered(1)),
            pl.BlockSpec((tt, PLE_DIM), lambda i, e: (i, 0), pipeline_mode=pl.Buffered(1)),
            _const_spec((1, D_MODEL)),
            _const_spec((D_MODEL, 2 * rows)),
            _const_spec((2 * PEER_HEADS, PEER_KEYS, LANES)),
            pl.BlockSpec((PEER_EC, D_MODEL), lambda i, e: (jnp.minimum(e, N_EC - 1), 0)),
            pl.BlockSpec((D_MODEL, PEER_EC), lambda i, e: (0, jnp.maximum(e - 1, 0))),
            _const_spec((1, D_MODEL)),
            _const_spec((D_MODEL, D_MODEL)),
            _const_spec((PLE_DIM, D_MODEL)),
            _const_spec((1, D_MODEL)),
            _const_spec((1, D_MODEL)),
        ],
        out_specs=pl.BlockSpec((tt, D_MODEL), lambda i, e: (i, 0)),
        out_shape=jax.ShapeDtypeStruct((t, D_MODEL), F32),
        scratch_shapes=[
            pltpu.VMEM((tt, D_MODEL), BF16),
            pltpu.VMEM((nl, rows, LANES), F32),
            pltpu.VMEM((nl, rows, LANES), F32),
            pltpu.VMEM((nl, rows, LANES), F32),
            pltpu.VMEM((nl, rows, LANES), F32),
            pltpu.VMEM((nl, PEER_EC, LANES), F32),
            pltpu.VMEM((nl, PEER_EC, LANES), F32),
            pltpu.VMEM((nl, PEER_EC, LANES), BF16),
            pltpu.VMEM((nl, D_MODEL, LANES), F32),
        ],
        compiler_params=pltpu.CompilerParams(
            dimension_semantics=("arbitrary", "arbitrary"),
            vmem_limit_bytes=VMEM_LIMIT),
        name="peer_ple",
    )(h1, p, w["norm_ffn"], w["wq"], w["keys"], w["u"], w["vt"],
      w["norm_ple_gate"], w["w_ple_gate"], w["w_ple_proj"], w["norm_ple_out"], w["norm_final"])


def _row(v):
    return v.reshape(1, -1).astype(F32)


def _prep_peer_weights(a):
    return {
        "norm_ffn": _row(a["norm_ffn"][0]),
        "wq": a["peer_wq"][0].astype(BF16),
        "keys": a["peer_keys"][0].reshape(2 * PEER_HEADS, PEER_KEYS, LANES).astype(BF16),
        "u": a["peer_u"][0].astype(BF16),
        "vt": a["peer_v"][0].astype(BF16).T,
        "norm_ple_gate": _row(a["norm_ple_gate"][0]),
        "w_ple_gate": a["w_ple_gate"][0].astype(BF16),
        "w_ple_proj": a["w_ple_proj"][0].astype(BF16),
        "norm_ple_out": _row(a["norm_ple_out"][0]),
        "norm_final": _row(a["norm_final"]),
    }


CHUNK = 128
MIX_TM = 512
HIST = SUBLANES


def _conv_from_ext(ext_ref, w, b, n):
    ext = ext_ref[0:HIST + n, :]
    y = b + ext[HIST:HIST + n, :] * w[CONV_W - 1:CONV_W, :]
    for d in range(1, CONV_W):
        y = y + pltpu.roll(ext, d, 0)[HIST:HIST + n, :] * w[CONV_W - 1 - d:CONV_W - d, :]
    return y


def _lru_coeffs(xl, gates):
    wr_ref, br_ref, wi_ref, bi_ref, lam_ref = gates
    xlb = xl.astype(BF16)

    def gate_dot(w_ref):
        return jnp.concatenate(
            [_dot(xlb[:, k * MXU_DIM:(k + 1) * MXU_DIM], w_ref[k]) for k in range(w_ref.shape[0])], axis=1)

    r = _sigmoid(gate_dot(wr_ref) + br_ref[...])
    ig = _sigmoid(gate_dot(wi_ref) + bi_ref[...])
    log_a = -LRU_C * r * _softplus(-lam_ref[...])
    a_t = jnp.exp(log_a)
    mult = jnp.sqrt(-_expm1(2.0 * log_a))
    return a_t, mult * (ig * xl)


def _ssd_chunk(xbc, dt, a, st_ref, consts, dsk):
    tri_ref, bdmask_ref = consts
    tri = tri_ref[...]
    da = dt * a
    cs = _dot3_left(tri, da)
    lane = lax.broadcasted_iota(jnp.int32, (CHUNK, LANES), 1)
    row = lax.broadcasted_iota(jnp.int32, (CHUNK, LANES), 0)
    causal = row >= lane
    low = lane < SSD_HEAD_DIM

    def per_channel(v):
        cols = [jnp.broadcast_to(v[:, h:h + 1], (CHUNK, LANES)) for h in range(SSD_HEADS)]
        wide = jnp.concatenate([jnp.where(low, cols[2 * k], cols[2 * k + 1]) for k in range(SSD_HEADS // 2)], axis=1)
        return cols, wide

    cs_cols, cs_e = per_channel(cs)
    _, dt_e = per_channel(dt)
    xs = xbc[:, 0:SSD_WIDTH]
    xdt = xs * dt_e
    bmat = xbc[:, SSD_WIDTH:SSD_WIDTH + LANES]
    cmat = xbc[:, SSD_WIDTH + LANES:SSD_WIDTH + 2 * LANES]
    last_e = cs_e[CHUNK - 1:CHUNK, :]
    xd = (xdt * jnp.exp(last_e - cs_e)).astype(BF16)
    st = st_ref[...]
    y_off = _dot(cmat.astype(BF16), st.astype(BF16)) * jnp.exp(cs_e)
    st_ref[...] = st * jnp.exp(last_e) + _dot(bmat.T.astype(BF16), xd) * bdmask_ref[...]

    cs_t = cs.T
    bmat_b = bmat.astype(BF16)
    cb = [_dot_nt(jnp.where(low, cmat, 0.0).astype(BF16), bmat_b),
          _dot_nt(jnp.where(low, 0.0, cmat).astype(BF16), bmat_b)]
    xdt_b = xdt.astype(BF16)
    pieces = []
    for hp in range(SSD_HEADS // 2):
        cols = slice(hp * LANES, (hp + 1) * LANES)
        outs = []
        for h in (2 * hp, 2 * hp + 1):
            seg = jnp.exp(cs_cols[h] - cs_t[h:h + 1, :])
            m = (cb[h // (SSD_HEADS // SSD_GROUPS)] * jnp.where(causal, seg, 0.0)).astype(BF16)
            outs.append(_dot(m, xdt_b[:, cols]))
        pieces.append(jnp.where(low, outs[0], outs[1]))
    y_diag = jnp.concatenate(pieces, axis=1)
    return y_diag + y_off + dsk * xs


def _mixer_out(x, y, z, y_lru, nssd_ref, wout_ref):
    y_ssd = _rms(y * _silu(z), nssd_ref[...])
    return (x + _dot(y_ssd.astype(BF16), wout_ref[0:SSD_WIDTH, :])
            + _dot(y_lru.astype(BF16), wout_ref[SSD_WIDTH:SSD_WIDTH + LRU_WIDTH, :]))


def _mix_p_kernel(x_ref, nmix_ref, wall_ref, cws_ref, cbs_ref, dtb_ref, alog_ref, dsk_ref, nssd_ref,
                  cwl_ref, cbl_ref, wr_ref, br_ref, wi_ref, bi_ref, lam_ref, nlru_ref, wout_ref,
                  tri_ref, bdmask_ref,
                  h1_ref, st_out_ref, sconv_out_ref, lstate_out_ref, lconv_out_ref,
                  exts_ref, extl_ref, st_ref, hcar_ref, y_ref, pa_ref, hb_ref):
    l = pl.program_id(1)
    tm = x_ref.shape[0]

    @pl.when(l == 0)
    def _():
        exts_ref[0:HIST, :] = jnp.zeros((HIST, SSD_CONV_DIM), F32)
        extl_ref[0:HIST, :] = jnp.zeros((HIST, LRU_WIDTH), F32)
        st_ref[...] = jnp.zeros_like(st_ref)
        hcar_ref[...] = jnp.zeros_like(hcar_ref)

    x = x_ref[...]
    xn = _rms(x, nmix_ref[...]).astype(BF16)

    def proj(lo, hi):
        return _dot(xn, wall_ref[:, lo:hi])

    exts_ref[HIST:HIST + tm, :] = proj(P_XBC, P_DT)
    xbc = _silu(_conv_from_ext(exts_ref, cws_ref[...], cbs_ref[...], tm))
    exts_ref[0:HIST, :] = exts_ref[tm:tm + HIST, :]
    dt = _softplus(proj(P_DT, P_GATE) + dtb_ref[...])
    a = -jnp.exp(alog_ref[...])
    consts = (tri_ref, bdmask_ref)
    extl_ref[HIST:HIST + tm, :] = proj(P_XL, P_ALL)
    for c in range(tm // CHUNK):
        rs = slice(c * CHUNK, (c + 1) * CHUNK)
        y_ref[rs, :] = _ssd_chunk(xbc[rs, :], dt[rs, :], a, st_ref, consts, dsk_ref[...])
    z = proj(P_Z, P_XBC)
    gate = proj(P_GATE, P_XL)

    xl = _conv_from_ext(extl_ref, cwl_ref[...], cbl_ref[...], tm)
    extl_ref[0:HIST, :] = extl_ref[tm:tm + HIST, :]
    a_t, bx = _lru_coeffs(xl, (wr_ref, br_ref, wi_ref, bi_ref, lam_ref))
    seg = tm // SUBLANES
    pitch = seg + SUBLANES
    hs_cols = []
    for j in range(LRU_WIDTH // LANES):
        cols = slice(j * LANES, (j + 1) * LANES)
        for s in range(SUBLANES):
            pa_ref[j, s * pitch:s * pitch + seg, :] = a_t[s * seg:(s + 1) * seg, cols]
            hb_ref[j, s * pitch:s * pitch + seg, :] = bx[s * seg:(s + 1) * seg, cols]
        hloc = jnp.zeros((SUBLANES, LANES), F32)
        ploc = jnp.ones((SUBLANES, LANES), F32)
        for tau in range(seg):
            rows = pl.ds(tau, SUBLANES, stride=pitch)
            a_r = pa_ref[j, rows, :]
            hloc = a_r * hloc + hb_ref[j, rows, :]
            ploc = a_r * ploc
            hb_ref[j, rows, :] = hloc
            pa_ref[j, rows, :] = ploc
        carry = hcar_ref[0:1, cols]
        parts = []
        for s in range(SUBLANES):
            rs = slice(s * pitch, s * pitch + seg)
            parts.append(hb_ref[j, rs, :] + pa_ref[j, rs, :] * carry)
            carry = hloc[s:s + 1, :] + ploc[s:s + 1, :] * carry
        hcar_ref[0:1, cols] = carry
        hs_cols.append(jnp.concatenate(parts, axis=0))
    hs = jnp.concatenate(hs_cols, axis=1)
    y_lru = _rms(hs * _gelu(gate), nlru_ref[...])

    h1_ref[...] = _mixer_out(x, y_ref[...], z, y_lru, nssd_ref, wout_ref)

    @pl.when(l == pl.num_programs(1) - 1)
    def _():
        st_out_ref[0] = st_ref[...]
        sconv_out_ref[0] = exts_ref[0:HIST, :]
        lstate_out_ref[0] = hcar_ref[...]
        lconv_out_ref[0] = extl_ref[0:HIST, :]


def _mix_prompt(x, w):
    b, l, _ = x.shape
    tm = MIX_TM
    assert l % tm == 0
    nl = l // tm
    names = ["norm_mix", "w_all", "conv_ssd_w", "conv_ssd_b", "dt_bias", "a_log", "d_skip", "norm_ssd",
             "conv_lru_w", "conv_lru_b", "w_r", "b_r", "w_i", "b_i", "lam", "norm_lru", "w_out",
             "tri", "bdmask"]
    ws = [w[n] for n in names]
    gs = SSD_GROUPS * SSD_STATE
    out_shape = (
        jax.ShapeDtypeStruct((b * l, D_MODEL), F32),
        jax.ShapeDtypeStruct((b, gs, SSD_WIDTH), F32),
        jax.ShapeDtypeStruct((b, HIST, SSD_CONV_DIM), F32),
        jax.ShapeDtypeStruct((b, SUBLANES, LRU_WIDTH), F32),
        jax.ShapeDtypeStruct((b, HIST, LRU_WIDTH), F32),
    )
    per_b = lambda shape: pl.BlockSpec((1,) + shape, lambda i, j: (i, 0, 0))
    return pl.pallas_call(
        _mix_p_kernel,
        grid=(b, nl),
        in_specs=[pl.BlockSpec((tm, D_MODEL), lambda i, j: (i * nl + j, 0))]
        + [_const_spec(v.shape) for v in ws],
        out_specs=(
            pl.BlockSpec((tm, D_MODEL), lambda i, j: (i * nl + j, 0)),
            per_b((gs, SSD_WIDTH)), per_b((HIST, SSD_CONV_DIM)),
            per_b((SUBLANES, LRU_WIDTH)), per_b((HIST, LRU_WIDTH)),
        ),
        out_shape=out_shape,
        scratch_shapes=[
            pltpu.VMEM((HIST + tm, SSD_CONV_DIM), F32),
            pltpu.VMEM((HIST + tm, LRU_WIDTH), F32),
            pltpu.VMEM((gs, SSD_WIDTH), F32),
            pltpu.VMEM((SUBLANES, LRU_WIDTH), F32),
            pltpu.VMEM((tm, SSD_WIDTH), F32),
            pltpu.VMEM((LRU_WIDTH // LANES, tm + SUBLANES * SUBLANES, LANES), F32),
            pltpu.VMEM((LRU_WIDTH // LANES, tm + SUBLANES * SUBLANES, LANES), F32),
        ],
        compiler_params=pltpu.CompilerParams(
            dimension_semantics=("arbitrary", "arbitrary"),
            vmem_limit_bytes=VMEM_LIMIT),
        name="mix_prompt",
    )(x.reshape(b * l, D_MODEL), *ws)


NS = 4
NB = 128


def _mix_s1_kernel(x_ref, bufs_ref, bufl_ref, lst_ref, nmix_ref, wall_ref, cws_ref, cbs_ref, dtb_ref,
                   cwl_ref, cbl_ref, wr_ref, br_ref, wi_ref, bi_ref, lam_ref, nlru_ref,
                   z_ref, xst_ref, bct_ref, dtt_ref, ylru_ref, sconv_ref, lstate_ref, lconv_ref):
    xn = _rms(x_ref[...], nmix_ref[...]).astype(BF16)
    proj = _dot(xn, wall_ref[...])
    z_ref[...] = proj[:, P_Z:P_XBC]

    def blocks(buf_ref, raw):
        return ([buf_ref[j * NB:(j + 1) * NB, :] for j in range(CONV_W - 1)]
                + [raw[t * NB:(t + 1) * NB, :] for t in range(NS)])

    def conv(xp, w, b, t):
        y = b + xp[t] * w[0:1, :]
        for k in range(1, CONV_W):
            y = y + xp[t + k] * w[k:k + 1, :]
        return y

    xp = blocks(bufs_ref, proj[:, P_XBC:P_DT])
    for j in range(CONV_W - 1):
        sconv_ref[j * NB:(j + 1) * NB, :] = xp[NS + j]
    for t in range(NS):
        xbc = _silu(conv(xp, cws_ref[...], cbs_ref[...], t))
        dt = _softplus(proj[t * NB:(t + 1) * NB, P_DT:P_GATE] + dtb_ref[...])
        xst_ref[t] = xbc[:, 0:SSD_WIDTH].T
        bct_ref[t] = xbc[:, SSD_WIDTH:SSD_CONV_DIM].T
        dtt_ref[t] = dt.T

    xp = blocks(bufl_ref, proj[:, P_XL:P_ALL])
    for j in range(CONV_W - 1):
        lconv_ref[j * NB:(j + 1) * NB, :] = xp[NS + j]
    h = lst_ref[...]
    for t in range(NS):
        rs = slice(t * NB, (t + 1) * NB)
        xl = conv(xp, cwl_ref[...], cbl_ref[...], t)
        a_t, bx = _lru_coeffs(xl, (wr_ref, br_ref, wi_ref, bi_ref, lam_ref))
        h = a_t * h + bx
        ylru_ref[rs, :] = _rms(h * _gelu(proj[rs, P_GATE:P_XL]), nlru_ref[...])
    lstate_ref[...] = h


def _mix_s1(x_tm, bufs_tm, bufl_tm, lstate, w):
    n = NS * NB
    names = ["norm_mix", "w_all", "conv_ssd_w", "conv_ssd_b", "dt_bias",
             "conv_lru_w", "conv_lru_b", "w_r", "b_r", "w_i", "b_i", "lam", "norm_lru"]
    out_shape = (
        jax.ShapeDtypeStruct((n, SSD_WIDTH), F32),
        jax.ShapeDtypeStruct((NS, SSD_WIDTH, NB), F32),
        jax.ShapeDtypeStruct((NS, SSD_CONV_DIM - SSD_WIDTH, NB), F32),
        jax.ShapeDtypeStruct((NS, LANES, NB), F32),
        jax.ShapeDtypeStruct((n, LRU_WIDTH), F32),
        jax.ShapeDtypeStruct(((CONV_W - 1) * NB, SSD_CONV_DIM), F32),
        jax.ShapeDtypeStruct((NB, LRU_WIDTH), F32),
        jax.ShapeDtypeStruct(((CONV_W - 1) * NB, LRU_WIDTH), F32),
    )
    return pl.pallas_call(
        _mix_s1_kernel, out_shape=out_shape,
        compiler_params=pltpu.CompilerParams(vmem_limit_bytes=VMEM_LIMIT),
        name="mix_sample_in",
    )(x_tm, bufs_tm, bufl_tm, lstate, *[w[k] for k in names])


def _ssd_s_kernel(xs_ref, bc_ref, dt_ref, alog_ref, dsk_ref, s_ref, y_ref, ns_ref):
    h = pl.program_id(0)
    g = h // (SSD_HEADS // SSD_GROUPS)
    b_rows = pl.ds(pl.multiple_of(g * SSD_STATE, SSD_STATE), SSD_STATE)
    c_rows = pl.ds(pl.multiple_of(LANES + g * SSD_STATE, SSD_STATE), SSD_STATE)
    a = -jnp.exp(alog_ref[0])
    ns_ref[...] = s_ref[...]
    for t in range(NS):
        dtr = dt_ref[t, 0]
        decay = jnp.exp(dtr * a)
        bt = bc_ref[t, b_rows, :]
        ct = bc_ref[t, c_rows, :]
        xs = xs_ref[t]
        xdt = xs * dtr
        for p in range(SSD_HEAD_DIM):
            s_p = decay * ns_ref[0, p] + xdt[p:p + 1, :] * bt
            ns_ref[0, p] = s_p
            y_ref[t, p:p + 1, :] = jnp.sum(ct * s_p, axis=0, keepdims=True)
        y_ref[t] = y_ref[t] + dsk_ref[0] * xs


def _ssd_sample(xst, bct, dtt, alog_b, dsk_b, state_bm):
    return pl.pallas_call(
        _ssd_s_kernel,
        grid=(SSD_HEADS,),
        in_specs=[
            pl.BlockSpec((NS, SSD_HEAD_DIM, NB), lambda h: (0, h, 0)),
            pl.BlockSpec((NS, SSD_CONV_DIM - SSD_WIDTH, NB), lambda h: (0, 0, 0)),
            pl.BlockSpec((NS, 1, 1, NB), lambda h: (0, h, 0, 0)),
            pl.BlockSpec((1, 1, NB), lambda h: (h, 0, 0)),
            pl.BlockSpec((1, 1, NB), lambda h: (h, 0, 0)),
            pl.BlockSpec((1, SSD_HEAD_DIM, SSD_STATE, NB), lambda h: (h, 0, 0, 0)),
        ],
        out_specs=(
            pl.BlockSpec((NS, SSD_HEAD_DIM, NB), lambda h: (0, h, 0)),
            pl.BlockSpec((1, SSD_HEAD_DIM, SSD_STATE, NB), lambda h: (h, 0, 0, 0)),
        ),
        out_shape=(
            jax.ShapeDtypeStruct((NS, SSD_WIDTH, NB), F32),
            jax.ShapeDtypeStruct((SSD_HEADS, SSD_HEAD_DIM, SSD_STATE, NB), F32),
        ),
        compiler_params=pltpu.CompilerParams(dimension_semantics=("arbitrary",),
                                             vmem_limit_bytes=VMEM_LIMIT),
        name="ssd_sample",
    )(xst, bct, dtt, alog_b, dsk_b, state_bm)


def _mix_s3_kernel(x_ref, yt_ref, z_ref, ylru_ref, nssd_ref, wout_ref, h1_ref):
    for t in range(NS):
        rs = slice(t * NB, (t + 1) * NB)
        h1_ref[rs, :] = _mixer_out(x_ref[rs, :], yt_ref[t].T, z_ref[rs, :], ylru_ref[rs, :],
                                   nssd_ref, wout_ref)


def _mix_s3(x_tm, yt, z, ylru, w):
    return pl.pallas_call(
        _mix_s3_kernel,
        out_shape=jax.ShapeDtypeStruct((NS * NB, D_MODEL), F32),
        compiler_params=pltpu.CompilerParams(vmem_limit_bytes=VMEM_LIMIT),
        name="mix_sample_out",
    )(x_tm, yt, z, ylru, w["norm_ssd"], w["w_out"])


def _pad_lanes(v):
    return jnp.pad(v.reshape(1, -1).astype(F32), ((0, 0), (0, LANES - v.size)))


def _gate_tiles(wb):
    k, n, _ = wb.shape
    per = MXU_DIM // n
    return jnp.stack([_block_diag(wb[c * per:(c + 1) * per]) for c in range(k // per)]).astype(BF16)


def _block_diag(wb):
    k, n, _ = wb.shape
    eye = jnp.eye(k, dtype=wb.dtype)
    return (eye[:, None, :, None] * wb[:, :, None, :]).reshape(k * n, k * n)


def _prep_mixer_weights(a):
    w_in = a["w_in"][0]
    s1 = SSD_WIDTH
    s2 = s1 + SSD_CONV_DIM
    s3 = s2 + SSD_HEADS
    s4 = s3 + LRU_WIDTH
    pad = jnp.zeros((D_MODEL, LANES - SSD_HEADS), w_in.dtype)
    w_all = jnp.concatenate([w_in[:, :s2], w_in[:, s2:s3], pad, w_in[:, s3:]], axis=1).astype(BF16)
    hd = jnp.arange(SSD_WIDTH) // SSD_HEAD_DIM
    grp_rows = jnp.arange(SSD_GROUPS * SSD_STATE) // SSD_STATE
    grp_cols = hd // (SSD_HEADS // SSD_GROUPS)
    bdmask = (grp_rows[:, None] == grp_cols[None, :]).astype(F32)
    tri = (jnp.arange(CHUNK)[:, None] >= jnp.arange(CHUNK)[None, :]).astype(BF16)
    return {
        "norm_mix": _row(a["norm_mix"][0]),
        "w_all": w_all,
        "conv_ssd_w": a["conv_ssd_w"][0].astype(F32),
        "conv_ssd_b": _row(a["conv_ssd_b"][0]),
        "dt_bias": _pad_lanes(a["dt_bias"][0]),
        "a_log": _pad_lanes(a["a_log"][0]),
        "d_skip": _row(jnp.repeat(a["d_skip"][0], SSD_HEAD_DIM)),
        "norm_ssd": _row(a["norm_ssd"][0]),
        "conv_lru_w": a["conv_lru_w"][0].astype(F32),
        "conv_lru_b": _row(a["conv_lru_b"][0]),
        "w_r": _gate_tiles(a["w_rgate"][0]),
        "b_r": _row(a["b_rgate"][0]),
        "w_i": _gate_tiles(a["w_igate"][0]),
        "b_i": _row(a["b_igate"][0]),
        "lam": _row(a["lru_lambda"][0]),
        "norm_lru": _row(a["norm_lru"][0]),
        "w_out": a["w_out"][0].astype(BF16),
        "tri": tri, "bdmask": bdmask,
        "a_log_b": jnp.broadcast_to(a["a_log"][0].astype(F32)[:, None, None], (SSD_HEADS, 1, NB)),
        "d_skip_b": jnp.broadcast_to(a["d_skip"][0].astype(F32)[:, None, None], (SSD_HEADS, 1, NB)),
    }


def _time_major(v):
    return jnp.swapaxes(v, 0, 1).reshape(-1, v.shape[-1])


def _seq_major(v, n):
    return jnp.swapaxes(v.reshape(n, NB, v.shape[-1]), 0, 1)


def _layer_prompt(x, p, wm, wp):
    b = x.shape[0]
    h1, st, sconv, lstate, lconv = _mix_prompt(x, wm)
    y = _peer_ple(h1, p.reshape(-1, PLE_DIM), wp).reshape(x.shape)
    st5 = st.reshape(b, SSD_GROUPS, SSD_STATE, SSD_HEADS, SSD_HEAD_DIM)
    hpg = SSD_HEADS // SSD_GROUPS
    ssd_state = jnp.stack([st5[:, h // hpg, :, h, :] for h in range(SSD_HEADS)], axis=1)
    ssd_state = jnp.swapaxes(ssd_state, 2, 3)
    keep = slice(HIST - (CONV_W - 1), HIST)
    return y, ssd_state, sconv[:, keep, :], lstate[:, 0, :], lconv[:, keep, :]


def _layer_sample(x, p, s_ssd, s_sconv, s_lru, s_lconv, wm, wp):
    x_tm = _time_major(x)
    z, xst, bct, dtt, ylru, sconv, lstate, lconv = _mix_s1(
        x_tm, _time_major(s_sconv), _time_major(s_lconv), s_lru, wm)
    dtt = dtt[:, :SSD_HEADS, :].reshape(NS, SSD_HEADS, 1, NB)
    state_bm = jnp.transpose(s_ssd, (1, 2, 3, 0))
    yt, new_bm = _ssd_sample(xst, bct, dtt, wm["a_log_b"], wm["d_skip_b"], state_bm)
    h1 = _mix_s3(x_tm, yt, z, ylru, wm)
    y = _seq_major(_peer_ple(h1, _time_major(p), wp), NS)
    ssd_state = jnp.transpose(new_bm, (3, 0, 1, 2))
    return y, ssd_state, _seq_major(sconv, CONV_W - 1), lstate, _seq_major(lconv, CONV_W - 1)


def kernel(x_prompt, x_sample, state_ssd, state_ssd_conv, state_lru, state_lru_conv, p_prompt, p_sample, norm_mix, w_in, conv_ssd_w, conv_ssd_b, dt_bias, a_log, d_skip, norm_ssd, conv_lru_w, conv_lru_b, w_rgate, b_rgate, w_igate, b_igate, lru_lambda, norm_lru, w_out, norm_ffn, peer_wq, peer_keys, peer_u, peer_v, norm_ple_gate, w_ple_gate, w_ple_proj, norm_ple_out, norm_final):
    assert x_sample.shape[:2] == (NB, NS) and norm_mix.shape[0] == 1
    a = dict(norm_mix=norm_mix, w_in=w_in, conv_ssd_w=conv_ssd_w, conv_ssd_b=conv_ssd_b, dt_bias=dt_bias,
             a_log=a_log, d_skip=d_skip, norm_ssd=norm_ssd, conv_lru_w=conv_lru_w, conv_lru_b=conv_lru_b,
             w_rgate=w_rgate, b_rgate=b_rgate, w_igate=w_igate, b_igate=b_igate, lru_lambda=lru_lambda,
             norm_lru=norm_lru, w_out=w_out, norm_ffn=norm_ffn, peer_wq=peer_wq, peer_keys=peer_keys,
             peer_u=peer_u, peer_v=peer_v, norm_ple_gate=norm_ple_gate, w_ple_gate=w_ple_gate,
             w_ple_proj=w_ple_proj, norm_ple_out=norm_ple_out, norm_final=norm_final)
    wm = _prep_mixer_weights(a)
    wp = _prep_peer_weights(a)
    yp, p_ssd, p_sconv, p_lru, p_lconv = _layer_prompt(x_prompt, p_prompt[0], wm, wp)
    ys, s_ssd, s_sconv, s_lru, s_lconv = _layer_sample(
        x_sample, p_sample[0], state_ssd[0], state_ssd_conv[0], state_lru[0], state_lru_conv[0], wm, wp)
    return (yp, ys, p_ssd[None], p_sconv[None], p_lru[None], p_lconv[None],
            s_ssd[None], s_sconv[None], s_lru[None], s_lconv[None])
```
